```python
import jax
import jax.numpy as jnp
from jax import lax
import numpy as np

D_MODEL = 1024
BATCH = 4
SEQ = 8192
DEPTH = 1

GRID_W = 64
N_MEM = 256
HEAD_DIM = 64
NA_HEADS = 6
NA_WIDTH = NA_HEADS * HEAD_DIM
NA_WIN_ROWS = 8
NA_WIN_COLS = 16
RW_HEADS = 6
RW_WIDTH = RW_HEADS * HEAD_DIM
DECAY_LORA = 64
AAA_LORA = 64
GATE_LORA = 128
MEM_HEADS = 4
MEM_WIDTH = MEM_HEADS * HEAD_DIM
N_BRANCH = 3
N_EXPERTS = 16
EC_CAPACITY = 2
EXPERT_FF = 2 * D_MODEL
RMS_EPS = 1e-6
GN_EPS = 64e-5
IN_WIDTHS = (NA_WIDTH, NA_WIDTH, NA_WIDTH, RW_WIDTH, RW_WIDTH, RW_WIDTH, DECAY_LORA, AAA_LORA, GATE_LORA, MEM_WIDTH, D_MODEL, D_MODEL, D_MODEL)
D_IN = sum(IN_WIDTHS)

kernel_name = 'hybrid_na_rwkv7_mem_ecmoe_encoder'


def rms_norm(x, g):
    xf = x.astype(jnp.float32)
    y = xf * lax.rsqrt(jnp.mean(xf * xf, axis=-1, keepdims=True) + RMS_EPS)
    return (y * g.astype(jnp.float32)).astype(x.dtype)


def split_columns(p):
    cuts = []
    acc = 0
    for w in IN_WIDTHS[:-1]:
        acc += w
        cuts.append(acc)
    return jnp.split(p, cuts, axis=-1)


def to_heads(t, n_heads):
    b, s, c = t.shape
    return t.reshape(b, s, n_heads, c // n_heads)


def neighborhood_attention(q, k, v, rpb):
    b, s, h, dh = q.shape
    rows = s // GRID_W
    kr = min(NA_WIN_ROWS, rows)
    kc = NA_WIN_COLS
    q = (q * (dh ** -0.5)).reshape(b, rows, GRID_W, h, dh).transpose(1, 0, 2, 3, 4)
    k = k.reshape(b, rows, GRID_W, h, dh)
    v = v.reshape(b, rows, GRID_W, h, dh)
    col = jnp.arange(GRID_W)
    col_start = jnp.clip(col - kc // 2, 0, GRID_W - kc)
    col_idx = col_start[:, None] + jnp.arange(kc)[None, :]
    col_off = col_idx - col[:, None] + (NA_WIN_COLS - 1)

    def row_block(args):
        r, q_r = args
        r_start = jnp.clip(r - kr // 2, 0, rows - kr)
        k_win = lax.dynamic_slice_in_dim(k, r_start, kr, axis=1)[:, :, col_idx]
        v_win = lax.dynamic_slice_in_dim(v, r_start, kr, axis=1)[:, :, col_idx]
        row_off = r_start + jnp.arange(kr) - r + (NA_WIN_ROWS - 1)
        bias = rpb[:, row_off[:, None, None], col_off[None, :, :]].transpose(0, 2, 1, 3)
        scores = jnp.einsum('bwhd,biwjhd->bhwij', q_r, k_win).astype(jnp.float32) + bias.astype(jnp.float32)[None]
        probs = jax.nn.softmax(scores.reshape(b, h, GRID_W, kr * kc), axis=-1).reshape(b, h, GRID_W, kr, kc)
        return jnp.einsum('bhwij,biwjhd->bwhd', probs.astype(v.dtype), v_win)

    out = lax.map(row_block, (jnp.arange(rows), q))
    return out.transpose(1, 0, 2, 3, 4).reshape(b, s, h * dh)


def token_shift(t, direction):
    if direction == 0:
        return jnp.pad(t[:, :-1], ((0, 0), (1, 0), (0, 0)))
    return jnp.pad(t[:, 1:], ((0, 0), (0, 1), (0, 0)))


def wkv7_scan(r, w, k, v, a, b, reverse):
    bsz, _, h, n = r.shape

    def step(state, inp):
        r_t, w_t, k_t, v_t, a_t, b_t = inp
        sa = jnp.einsum('bhvk,bhk->bhv', state, a_t)
        state = state * w_t[:, :, None, :] + sa[..., None] * b_t[:, :, None, :] + v_t[..., None] * k_t[:, :, None, :]
        return state, jnp.einsum('bhvk,bhk->bhv', state, r_t)

    xs = tuple(t.transpose(1, 0, 2, 3) for t in (r, w, k, v, a, b))
    _, y = lax.scan(step, jnp.zeros((bsz, h, n, n), jnp.float32), xs, reverse=reverse)
    return y.transpose(1, 0, 2, 3)


def rwkv7_bidirectional(p_r, p_k, p_v, lat_w, lat_a, lat_g, mu_rkv, mu_w, mu_a, w0, w2, a0, a2, k_k, k_a, r_k, g2, ln_g, ln_b):
    out_dtype = p_r.dtype
    f32 = lambda t: t.astype(jnp.float32)
    p_r, p_k, p_v, lat_w, lat_a, lat_g = (f32(t) for t in (p_r, p_k, p_v, lat_w, lat_a, lat_g))
    mu_rkv, mu_w, mu_a, w0, w2, a0, a2 = (f32(t) for t in (mu_rkv, mu_w, mu_a, w0, w2, a0, a2))
    k_k, k_a, r_k, g2, ln_g, ln_b = (f32(t) for t in (k_k, k_a, r_k, g2, ln_g, ln_b))
    bsz, s, c = p_r.shape
    y_sum = jnp.zeros((bsz, s, RW_HEADS, HEAD_DIM), jnp.float32)
    bonus_sum = jnp.zeros((bsz, s, RW_HEADS, HEAD_DIM), jnp.float32)
    for d in range(2):
        def mix(t, mu):
            return t + (token_shift(t, d) - t) * mu
        r = mix(p_r, mu_rkv[d, 0])
        k = mix(p_k, mu_rkv[d, 1])
        v = mix(p_v, mu_rkv[d, 2])
        w_log = -jax.nn.softplus(-(w0[d] + jnp.tanh(mix(lat_w, mu_w[d])) @ w2[d])) - 0.5
        decay = jnp.exp(-jnp.exp(w_log))
        a = jax.nn.sigmoid(a0[d] + mix(lat_a, mu_a[d]) @ a2[d])
        kk = to_heads(k * k_k, RW_HEADS)
        kk = kk / jnp.maximum(jnp.sqrt(jnp.sum(kk * kk, axis=-1, keepdims=True)), 1e-12)
        k = k * (1.0 + (a - 1.0) * k_a)
        rh, kh, vh, ah = (to_heads(t, RW_HEADS) for t in (r, k, v, a))
        y_sum = y_sum + wkv7_scan(rh, to_heads(decay, RW_HEADS), kh, vh, -kk, kk * ah, reverse=(d == 1))
        bonus_sum = bonus_sum + jnp.sum(rh * kh * r_k, axis=-1, keepdims=True) * vh
    mean = jnp.mean(y_sum, axis=-1, keepdims=True)
    var = jnp.mean(jnp.square(y_sum - mean), axis=-1, keepdims=True)
    y = ((y_sum - mean) * lax.rsqrt(var + GN_EPS)).reshape(bsz, s, c) * ln_g + ln_b
    y = y + bonus_sum.reshape(bsz, s, c)
    g = jax.nn.sigmoid(lat_g) @ g2
    return (y * g).astype(out_dtype)


def memory_attention(q, mem_h, w_mem_kv):
    b, s, _ = q.shape
    mk, mv = jnp.split(mem_h @ w_mem_kv, 2, axis=-1)
    qh = to_heads(q, MEM_HEADS) * (HEAD_DIM ** -0.5)
    kh = to_heads(mk, MEM_HEADS)
    vh = to_heads(mv, MEM_HEADS)
    scores = jnp.einsum('bshd,bmhd->bhsm', qh, kh).astype(jnp.float32)
    probs = jax.nn.softmax(scores, axis=-1)
    return jnp.einsum('bhsm,bmhd->bshd', probs.astype(vh.dtype), vh).reshape(b, s, MEM_WIDTH)


def expert_choice_ffn(h, w_router, w_gate, w_up, w_down):
    b, s, d = h.shape
    cap = EC_CAPACITY * s // N_EXPERTS
    affinity = jax.nn.softmax((h @ w_router).astype(jnp.float32), axis=-1)
    top_val, top_idx = lax.top_k(affinity.transpose(0, 2, 1), cap)
    xe = jax.vmap(lambda hb, ib: hb[ib])(h, top_idx)
    act = jax.nn.silu(jnp.einsum('becd,edf->becf', xe, w_gate)) * jnp.einsum('becd,edf->becf', xe, w_up)
    ye = jnp.einsum('becf,efd->becd', act, w_down) * top_val[..., None].astype(h.dtype)
    return jax.vmap(lambda yb, ib: jnp.zeros((s, d), yb.dtype).at[ib.reshape(-1)].add(yb.reshape(-1, d)))(ye, top_idx)


def setup_inputs(seed: int = 0) -> dict:
    key = jax.random.key(seed)
    ks = iter(jax.random.split(key, 40))
    L = DEPTH

    def nrm(shape, scale):
        return scale * jax.random.normal(next(ks), shape, jnp.float32)

    def uni(shape):
        return jax.random.uniform(next(ks), shape, jnp.float32, 0.0, 1.0)

    decay_ramp = jnp.linspace(-6.5, -1.5, RW_WIDTH, dtype=jnp.float32)
    return {
        'x': nrm((BATCH, SEQ, D_MODEL), 1.0),
        'mem': nrm((BATCH, N_MEM, D_MODEL), 1.0),
        'norm_mix_g': 1.0 + nrm((L, D_MODEL), 0.02),
        'norm_mem_g': 1.0 + nrm((L, D_MODEL), 0.02),
        'w_in': nrm((L, D_MODEL, D_IN), D_MODEL ** -0.5),
        'na_rpb': nrm((L, NA_HEADS, 2 * NA_WIN_ROWS - 1, 2 * NA_WIN_COLS - 1), 0.05),
        'rw_mu_rkv': uni((L, 2, 3, RW_WIDTH)),
        'rw_mu_w': uni((L, 2, DECAY_LORA)),
        'rw_mu_a': uni((L, 2, AAA_LORA)),
        'rw_w0': decay_ramp + nrm((L, 2, RW_WIDTH), 0.1),
        'rw_w2': nrm((L, 2, DECAY_LORA, RW_WIDTH), 0.1),
        'rw_a0': nrm((L, 2, RW_WIDTH), 0.1),
        'rw_a2': nrm((L, 2, AAA_LORA, RW_WIDTH), 0.1),
        'rw_k_k': 0.85 + nrm((L, RW_WIDTH), 0.02),
        'rw_k_a': 1.0 + nrm((L, RW_WIDTH), 0.02),
        'rw_r_k': nrm((L, RW_HEADS, HEAD_DIM), 0.1),
        'rw_g2': nrm((L, GATE_LORA, RW_WIDTH), GATE_LORA ** -0.5),
        'rw_ln_g': 1.0 + nrm((L, RW_WIDTH), 0.02),
        'rw_ln_b': nrm((L, RW_WIDTH), 0.02),
        'w_mem_kv': nrm((L, D_MODEL, 2 * MEM_WIDTH), D_MODEL ** -0.5),
        'w_branch_na': nrm((L, NA_WIDTH, D_MODEL), NA_WIDTH ** -0.5),
        'w_branch_rw': nrm((L, RW_WIDTH, D_MODEL), RW_WIDTH ** -0.5),
        'w_branch_mem': nrm((L, MEM_WIDTH, D_MODEL), MEM_WIDTH ** -0.5),
        'w_out': nrm((L, D_MODEL, D_MODEL), D_MODEL ** -0.5),
        'norm_ffn_g': 1.0 + nrm((L, D_MODEL), 0.02),
        'w_router': nrm((L, D_MODEL, N_EXPERTS), D_MODEL ** -0.5),
        'w_exp_gate': nrm((L, N_EXPERTS, D_MODEL, EXPERT_FF), D_MODEL ** -0.5),
        'w_exp_up': nrm((L, N_EXPERTS, D_MODEL, EXPERT_FF), D_MODEL ** -0.5),
        'w_exp_down': nrm((L, N_EXPERTS, EXPERT_FF, D_MODEL), EXPERT_FF ** -0.5),
        'norm_final_g': 1.0 + nrm((D_MODEL,), 0.02),
    }


def reference(x, mem, norm_mix_g, norm_mem_g, w_in, na_rpb, rw_mu_rkv, rw_mu_w, rw_mu_a, rw_w0, rw_w2, rw_a0, rw_a2, rw_k_k, rw_k_a, rw_r_k, rw_g2, rw_ln_g, rw_ln_b, w_mem_kv, w_branch_na, w_branch_rw, w_branch_mem, w_out, norm_ffn_g, w_router, w_exp_gate, w_exp_up, w_exp_down, norm_final_g):
    for l in range(DEPTH):
        h = rms_norm(x, norm_mix_g[l])
        (na_q, na_k, na_v, p_r, p_k, p_v, lat_w, lat_a, lat_g, mem_q,
         gate_na, gate_rw, gate_mem) = split_columns(h @ w_in[l])
        y_na = neighborhood_attention(to_heads(na_q, NA_HEADS), to_heads(na_k, NA_HEADS), to_heads(na_v, NA_HEADS), na_rpb[l])
        y_rw = rwkv7_bidirectional(p_r, p_k, p_v, lat_w, lat_a, lat_g, rw_mu_rkv[l], rw_mu_w[l], rw_mu_a[l], rw_w0[l], rw_w2[l], rw_a0[l], rw_a2[l], rw_k_k[l], rw_k_a[l], rw_r_k[l], rw_g2[l], rw_ln_g[l], rw_ln_b[l])
        y_mem = memory_attention(mem_q, rms_norm(mem, norm_mem_g[l]), w_mem_kv[l])
        merged = (jax.nn.sigmoid(gate_na) * (y_na @ w_branch_na[l])
                  + jax.nn.sigmoid(gate_rw) * (y_rw @ w_branch_rw[l])
                  + jax.nn.sigmoid(gate_mem) * (y_mem @ w_branch_mem[l]))
        x = x + merged @ w_out[l]
        x = x + expert_choice_ffn(rms_norm(x, norm_ffn_g[l]), w_router[l], w_exp_gate[l], w_exp_up[l], w_exp_down[l])
    return rms_norm(x, norm_final_g)
```

```python
import functools

import jax
import jax.numpy as jnp
from jax import lax
from jax.experimental import pallas as pl
from jax.experimental.pallas import tpu as pltpu

F32 = jnp.float32
BF16 = jnp.bfloat16
I32 = jnp.int32
U32 = jnp.uint32
HI = lax.Precision.HIGHEST

HEAD_DIM = 64
GRID_W = 64
NA_WIN_ROWS = 8
NA_WIN_COLS = 16
EC_CAPACITY = 2
RMS_EPS = 1e-6
GN_EPS = 64e-5
NEG_BIG = -1e30

LANES = 128
CHUNK = 64
SCATTER_GROUP = 8
VMEM_LIMIT = 56 * 1024 * 1024


def _cp(sem, vmem=VMEM_LIMIT):
    return pltpu.CompilerParams(dimension_semantics=sem, vmem_limit_bytes=vmem)


def _dot(a, b, prec=None):
    return jnp.dot(a, b, preferred_element_type=F32, precision=prec)


def _dot_nt(a, b, prec=None):
    return lax.dot_general(a, b, (((1,), (1,)), ((), ())), preferred_element_type=F32, precision=prec)


def _dot_tn(a, b, prec=None):
    return lax.dot_general(a, b, (((0,), (0,)), ((), ())), preferred_element_type=F32, precision=prec)


def _sigmoid(x):
    return 1.0 / (1.0 + jnp.exp(-x))


def _split2(x):
    hi = x.astype(BF16)
    return hi, (x - hi.astype(F32)).astype(BF16)


def _split3(x):
    hi, rest = x.astype(BF16), x - x.astype(BF16).astype(F32)
    mid = rest.astype(BF16)
    return hi, mid, (rest - mid.astype(F32)).astype(BF16)


def _dot_split_lhs(x, w_bf):
    n = x.shape[0]
    r = _dot(jnp.concatenate(_split2(x), axis=0), w_bf)
    return r[:n] + r[n:]


def _iota(shape, dim):
    return lax.broadcasted_iota(I32, shape, dim)


def _inproj_kernel(x_ref, g_ref, w_ref, q_ref, k_ref, v_ref, rw_ref, memq_ref, gates_ref, *, cuts):
    x = x_ref[...]
    ms = jnp.mean(x * x, axis=-1, keepdims=True)
    h = (x * lax.rsqrt(ms + RMS_EPS) * g_ref[...]).astype(BF16)
    c = cuts
    q_ref[...] = _dot(h, w_ref[:, c[0]:c[1]]).astype(BF16)
    k_ref[...] = _dot(h, w_ref[:, c[1]:c[2]]).astype(BF16)
    v_ref[...] = _dot(h, w_ref[:, c[2]:c[3]]).astype(BF16)
    rw_ref[...] = _dot(h, w_ref[:, c[3]:c[4]])
    memq_ref[...] = _dot(h, w_ref[:, c[4]:c[5]]).astype(BF16)
    gates_ref[...] = _sigmoid(_dot(h, w_ref[:, c[5]:c[6]])).astype(BF16)


def _inproj(x2, g, w_bf, na_w, rw_w, mem_w, d_model, tm):
    n = x2.shape[0]
    d_in = w_bf.shape[1]
    cuts = (0, na_w, 2 * na_w, 3 * na_w, 3 * na_w + rw_w, 3 * na_w + rw_w + mem_w, d_in)
    assert cuts[6] - cuts[5] == 3 * d_model
    row = lambda w: pl.BlockSpec((tm, w), lambda i: (i, 0))
    return pl.pallas_call(
        functools.partial(_inproj_kernel, cuts=cuts),
        grid=(n // tm,),
        in_specs=[row(d_model), pl.BlockSpec((1, d_model), lambda i: (0, 0)),
                  pl.BlockSpec((d_model, d_in), lambda i: (0, 0))],
        out_specs=[row(na_w), row(na_w), row(na_w), row(rw_w), row(mem_w), row(3 * d_model)],
        out_shape=[jax.ShapeDtypeStruct((n, na_w), BF16)] * 3
        + [jax.ShapeDtypeStruct((n, rw_w), F32), jax.ShapeDtypeStruct((n, mem_w), BF16),
           jax.ShapeDtypeStruct((n, 3 * d_model), BF16)],
        compiler_params=_cp(("parallel",)),
    )(x2, g, w_bf)


def _na_bias_kernel(rpb_ref, o_ref):
    w, kc = GRID_W, NA_WIN_COLS
    shape = (rpb_ref.shape[1], w * w)
    o = _iota(shape, 0)
    flat = _iota(shape, 1)
    qc = flat // w
    c = flat % w
    cs = jnp.clip(qc - kc // 2, 0, w - kc)
    valid = (c >= cs) & (c < cs + kc)
    onehot = ((c - qc + (kc - 1) == o) & valid).astype(F32)
    o_ref[...] = _dot(rpb_ref[...], onehot, HI) + jnp.where(valid[0:1], 0.0, NEG_BIG)


def _na_bias_table(rpb):
    kr, w = NA_WIN_ROWS, GRID_W
    h, n_ro, n_co = rpb.shape
    rows = -(-h * n_ro // 8) * 8
    rpb2 = jnp.zeros((rows, LANES), F32).at[:h * n_ro, :n_co].set(rpb.reshape(h * n_ro, n_co).astype(F32))
    tab = pl.pallas_call(
        _na_bias_kernel,
        out_shape=jax.ShapeDtypeStruct((rows, w * w), F32),
    )(rpb2)[:h * n_ro].reshape(h, n_ro, w, w)
    t = jnp.stack([tab[:, kr - 1 - dl:2 * kr - 1 - dl] for dl in range(kr)], axis=1)
    return t.transpose(0, 1, 3, 2, 4).reshape(h, kr, w, kr * w)


def _na_kernel(q_ref, k_ref, v_ref, bias_ref, o_ref, *, tr, rows, n_pairs):
    j = pl.program_id(1)
    w = GRID_W
    slab = NA_WIN_ROWS * w
    lane = _iota((w, LANES), 1)
    lo_mask = lane < HEAD_DIM

    def row(i, carry):
        r = j * tr + i
        rs = jnp.clip(r - NA_WIN_ROWS // 2, 0, rows - NA_WIN_ROWS)
        delta = r - rs
        q = q_ref[0, pl.ds(pl.multiple_of(i * w, w), w), :] * (HEAD_DIM ** -0.5)
        k0 = pl.multiple_of(rs * w, w)
        kk = k_ref[0, pl.ds(k0, slab), :]
        vv = v_ref[0, pl.ds(k0, slab), :]
        heads = [(p, hh) for p in range(n_pairs) for hh in range(2)]
        pair = lambda t, p: t[:, LANES * p:LANES * (p + 1)]
        zero = jnp.zeros((w, LANES), q.dtype)
        qm = [jnp.where(lo_mask if hh == 0 else ~lo_mask, pair(q, p), zero) for p, hh in heads]
        s = [_dot_nt(qmi, pair(kk, p)) + bias_ref[2 * p + hh, delta] for qmi, (p, hh) in zip(qm, heads)]
        e = [jnp.exp(si - jnp.max(si, axis=-1, keepdims=True)) for si in s]
        o = [_dot(ei.astype(BF16), pair(vv, p)) / jnp.sum(ei, axis=-1, keepdims=True)
             for ei, (p, _) in zip(e, heads)]
        outs = [jnp.where(lo_mask, o[2 * p], o[2 * p + 1]) for p in range(n_pairs)]
        o_ref[0, pl.ds(pl.multiple_of(i * w, w), w), :] = jnp.concatenate(outs, axis=-1).astype(o_ref.dtype)
        return carry

    lax.fori_loop(0, tr, row, 0, unroll=2)


def _na_attention(q, k, v, bias, tr):
    b, s, c = q.shape
    rows = s // GRID_W
    n_heads = c // HEAD_DIM
    assert n_heads % 2 == 0 and rows >= NA_WIN_ROWS and rows % tr == 0
    return pl.pallas_call(
        functools.partial(_na_kernel, tr=tr, rows=rows, n_pairs=n_heads // 2),
        grid=(b, rows // tr),
        in_specs=[pl.BlockSpec((1, tr * GRID_W, c), lambda bi, j: (bi, j, 0)),
                  pl.BlockSpec((1, s, c), lambda bi, j: (bi, 0, 0)),
                  pl.BlockSpec((1, s, c), lambda bi, j: (bi, 0, 0)),
                  pl.BlockSpec(bias.shape, lambda bi, j: (0, 0, 0, 0))],
        out_specs=pl.BlockSpec((1, tr * GRID_W, c), lambda bi, j: (bi, j, 0)),
        out_shape=jax.ShapeDtypeStruct((b, s, c), BF16),
        compiler_params=_cp(("parallel", "arbitrary")),
    )(q, k, v, bias)


def _softplus(z):
    return jnp.maximum(z, 0.0) + jnp.log(1.0 + jnp.exp(-jnp.abs(z)))


def _stack_pair(xp, lo_mask):
    zero = jnp.zeros_like(xp)
    return jnp.concatenate([jnp.where(lo_mask, xp, zero), jnp.where(lo_mask, zero, xp)], axis=0)


def _rwkv_prep(d, x, edge, prm, rw_w):
    (mu_rkv, mu_wa, w0, wl_hi, a0, wl_lo, k_k, k_a, r_k, bd) = prm
    L = CHUNK
    rowi = _iota((L, 1), 0)
    if d == 0:
        xs = jnp.where(rowi == 0, edge, pltpu.roll(x, 1, 0))
    else:
        xs = jnp.where(rowi == L - 1, edge, pltpu.roll(x, L - 1, 0))

    def mix(lo, hi, mu):
        t = x[:, lo:hi]
        return t + (xs[:, lo:hi] - t) * mu

    r = mix(0, rw_w, mu_rkv[d, 0:1])
    k = mix(rw_w, 2 * rw_w, mu_rkv[d, 1:2])
    v = mix(2 * rw_w, 3 * rw_w, mu_rkv[d, 2:3])
    lwa = mix(3 * rw_w, 3 * rw_w + LANES, mu_wa[d:d + 1])
    lane = _iota((L, LANES), 1)
    tw = jnp.where(lane < HEAD_DIM, jnp.tanh(lwa), lwa)
    tw_hi, tw_lo = _split2(tw)
    hh = _dot(jnp.concatenate([tw_hi, tw_lo], axis=0), wl_hi[d])
    lora = hh[:L] + hh[L:] + _dot(tw_hi, wl_lo[d])
    yield
    w_log = -_softplus(-(w0[d:d + 1] + lora[:, :rw_w])) - 0.5
    logw = -jnp.exp(w_log)
    a = _sigmoid(a0[d:d + 1] + lora[:, rw_w:])
    kk = k * k_k
    k = k * (1.0 + (a - 1.0) * k_a)
    sums = _dot(jnp.concatenate(_split2(kk * kk) + _split2(r * k * r_k), axis=0), bd)
    ti = _iota((L, L), 0)
    si = _iota((L, L), 1)
    tri = ((si <= ti) if d == 0 else (si >= ti)).astype(BF16)
    lc3 = _dot(tri, jnp.concatenate(_split3(logw), axis=1))
    yield
    kk = kk / jnp.maximum(jnp.sqrt(sums[:L] + sums[L:2 * L]), 1e-12)
    bonus = (sums[2 * L:3 * L] + sums[3 * L:]) * v
    bv = kk * a
    lc = lc3[:, :rw_w] + lc3[:, rw_w:2 * rw_w] + lc3[:, 2 * rw_w:]
    tot = jnp.sum(logw, axis=0, keepdims=True)
    e_neg = jnp.exp(-lc)
    e_end = jnp.exp(tot - lc)
    ops = dict(r=r * jnp.exp(lc), a=-kk * jnp.exp(lc - logw), b=bv * e_neg, k=k * e_neg, v=v, bh=bv * e_end,
               kh=k * e_end)
    return ops, jnp.exp(tot), bonus


RWKV_OPERANDS = ("a", "r", "b", "k", "v", "bh", "kh")


def _advance(gens, segments):
    for _ in range(segments):
        for g in gens:
            next(g)


def _finish(gen):
    try:
        while True:
            next(gen)
    except StopIteration as stop:
        return stop.value


def _rwkv_stack(preps, n_pairs):
    lo_mask = _iota((CHUNK, LANES), 1) < HEAD_DIM
    inst = [(d, p) for d in range(len(preps)) for p in range(n_pairs)]
    ops = [[_stack_pair(preps[d][0][name][:, LANES * p:LANES * (p + 1)].astype(BF16), lo_mask) for d, p in inst]
           for name in RWKV_OPERANDS]
    gam = [preps[d][1][:, LANES * p:LANES * (p + 1)] for d, p in inst]
    return ops, gam


def _rwkv_blocks(ops, gam, state_ref, n_dirs, n_pairs):
    L = CHUNK
    n2 = 2 * L
    ii = _iota((n2, n2), 0)
    jj = _iota((n2, n2), 1)
    strict = ((jj < ii), (jj > ii))
    incl = ((jj <= ii), (jj >= ii))
    eye = (ii == jj).astype(F32)
    zero = jnp.zeros((n2, n2), F32)
    inst = [(d, p) for d in range(n_dirs) for p in range(n_pairs)]
    each = lambda f, *ls: [f(*a) for a in zip(*ls)]
    a_s, r_s, b_s, k_s, v_s, bh_s, kh_s = ops
    g = each(lambda a, r, b, k: _dot_nt(jnp.concatenate([a, r], axis=0), jnp.concatenate([b, k], axis=0)),
             a_s, r_s, b_s, k_s)
    n_ab = [jnp.where(strict[d], gi[:n2, :n2], zero) for (d, _), gi in zip(inst, g)]
    a_ak = [jnp.where(strict[d], gi[:n2, n2:], zero).astype(BF16) for (d, _), gi in zip(inst, g)]
    a_rb = [jnp.where(incl[d], gi[n2:, :n2], zero).astype(BF16) for (d, _), gi in zip(inst, g)]
    a_rk = [jnp.where(incl[d], gi[n2:, n2:], zero).astype(BF16) for (d, _), gi in zip(inst, g)]
    yield
    av = each(lambda ak, rk, v: _dot(jnp.concatenate([ak, rk], axis=0), v), a_ak, a_rk, v_s)
    t_inv = [eye + n for n in n_ab]
    npow = each(lambda n: _dot(n, n), [n.astype(BF16) for n in n_ab])
    yield
    for _ in range(CHUNK.bit_length() - 3):
        nb = [n.astype(BF16) for n in npow]
        both = each(lambda t, n: _dot(jnp.concatenate([t.astype(BF16), n], axis=0), n), t_inv, nb)
        t_inv = each(lambda t, x: t + x[:n2], t_inv, both)
        npow = [x[n2:] for x in both]
        yield
    t_inv = each(lambda t, n: t + _dot(t.astype(BF16), n.astype(BF16)), t_inv, npow)
    pq = each(lambda t, a, x: _dot(t.astype(BF16), jnp.concatenate([a, x[:n2].astype(BF16)], axis=1)).astype(BF16),
              t_inv, a_s, av)
    yield
    arb_pq = each(_dot, a_rb, pq)
    ry = each(lambda r, x: (r.astype(F32) + x[:, :LANES]).astype(BF16), r_s, arb_pq)
    y0 = each(lambda x, a: x[:, LANES:] + a[n2:], arb_pq, av)
    yield
    pq_bh = each(_dot_tn, pq, bh_s)
    m_mat = each(lambda gm, x: (eye * gm + x[:LANES]).astype(BF16), gam, pq_bh)
    c_mat = each(lambda x, v, kh: x[LANES:] + _dot_tn(v, kh), pq_bh, v_s, kh_s)
    yield
    s_old = [state_ref[i].astype(BF16) for i in range(len(inst))]
    y = each(lambda r, s, y_: _dot_nt(r, s) + y_, ry, s_old, y0)
    s_new = each(lambda s, m, c: _dot(s, m) + c, s_old, m_mat, c_mat)
    for i, s in enumerate(s_new):
        state_ref[i] = s
    return [jnp.concatenate([yi[:L] + yi[L:] for (di, _), yi in zip(inst, y) if di == d], axis=-1)
            for d in range(n_dirs)]


def _rwkv_kernel(cur0_ref, prev0_ref, cur1_ref, next1_ref, mu_rkv_ref, mu_wa_ref, w0_ref, wlhi_ref, a0_ref,
                 wllo_ref, kk_ref, ka_ref, rk_ref, bd_ref, y0_ref, y1_ref, b0_ref, b1_ref, ops_ref, gam_ref,
                 state_ref, *, n_pairs, rw_w):
    c = pl.program_id(1)

    @pl.when(c == 0)
    def _():
        ops_ref[...] = jnp.zeros_like(ops_ref)
        gam_ref[...] = jnp.zeros_like(gam_ref)
        state_ref[...] = jnp.zeros_like(state_ref)

    n_inst = 2 * n_pairs
    staged = [[ops_ref[n, i] for i in range(n_inst)] for n in range(len(RWKV_OPERANDS))]
    staged_gam = [gam_ref[i] for i in range(n_inst)]

    prm = (mu_rkv_ref[...], mu_wa_ref[...], w0_ref[...], wlhi_ref[...], a0_ref[...], wllo_ref[...],
           kk_ref[...], ka_ref[...], rk_ref[...], bd_ref[...])
    first = jnp.minimum(c, pl.num_programs(1) - 2) == 0
    edge0 = jnp.where(first, 0.0, prev0_ref[0, 7:8, :])
    edge1 = jnp.where(first, 0.0, next1_ref[0, 0:1, :])
    prep = [_rwkv_prep(0, cur0_ref[0], edge0, prm, rw_w), _rwkv_prep(1, cur1_ref[0], edge1, prm, rw_w)]
    blocks = _rwkv_blocks(staged, staged_gam, state_ref, 2, n_pairs)
    _advance(prep, 1)
    _advance([blocks], 2)
    _advance(prep, 1)
    _advance([blocks], 4)
    preps = [_finish(g) for g in prep]
    b0_ref[0] = preps[0][2]
    b1_ref[0] = preps[1][2]
    ops, gam = _rwkv_stack(preps, n_pairs)
    for n, per_inst in enumerate(ops):
        for i, t in enumerate(per_inst):
            ops_ref[n, i] = t
    for i, t in enumerate(gam):
        gam_ref[i] = t

    ys = _finish(blocks)
    y0_ref[0] = ys[0]
    y1_ref[0] = ys[1]


def _head_block_diag(width):
    h = jnp.arange(width) // HEAD_DIM
    return (h[:, None] == h[None, :]).astype(F32)


def _rwkv(rw, mu_rkv, mu_w, mu_a, w0, w2, a0, a2, k_k, k_a, r_k, rw_w):
    b, s, cols = rw.shape
    assert s % CHUNK == 0 and rw_w % LANES == 0 and mu_w.shape[-1] == HEAD_DIM and mu_a.shape[-1] == HEAD_DIM
    nc = s // CHUNK
    n_pairs = rw_w // LANES
    sub = CHUNK // 8
    mu_wa = jnp.concatenate([mu_w, mu_a], axis=-1)
    wl = jnp.concatenate([jnp.concatenate([w2, jnp.zeros_like(w2)], axis=1),
                          jnp.concatenate([jnp.zeros_like(a2), a2], axis=1)], axis=2)
    wl_hi, wl_lo = _split2(wl)
    bd = _head_block_diag(rw_w).astype(BF16)
    full = lambda a: pl.BlockSpec(a.shape, lambda bi, c: (0,) * a.ndim)
    params = [mu_rkv, mu_wa, w0, wl_hi, a0, wl_lo, k_k.reshape(1, rw_w), k_a.reshape(1, rw_w),
              r_k.reshape(1, rw_w), bd]
    out = jax.ShapeDtypeStruct((b, s, rw_w), F32)
    cp = lambda c: jnp.minimum(c, nc - 1)
    cm = lambda c: jnp.maximum(c - 1, 0)
    blk = lambda chunk_of, rev: pl.BlockSpec(
        (1, CHUNK, rw_w), lambda bi, c: (bi, nc - 1 - chunk_of(c) if rev else chunk_of(c), 0))
    return pl.pallas_call(
        functools.partial(_rwkv_kernel, n_pairs=n_pairs, rw_w=rw_w),
        grid=(b, nc + 1),
        in_specs=[pl.BlockSpec((1, CHUNK, cols), lambda bi, c: (bi, cp(c), 0)),
                  pl.BlockSpec((1, 8, cols), lambda bi, c: (bi, jnp.maximum(cp(c) * sub - 1, 0), 0)),
                  pl.BlockSpec((1, CHUNK, cols), lambda bi, c: (bi, nc - 1 - cp(c), 0)),
                  pl.BlockSpec((1, 8, cols),
                               lambda bi, c: (bi, jnp.minimum((nc - cp(c)) * sub, nc * sub - 1), 0))]
        + [full(a) for a in params],
        out_specs=[blk(cm, False), blk(cm, True), blk(cp, False), blk(cp, True)],
        out_shape=[out] * 4,
        scratch_shapes=[pltpu.VMEM((len(RWKV_OPERANDS), 2 * n_pairs, 2 * CHUNK, LANES), BF16),
                        pltpu.VMEM((2 * n_pairs, 1, LANES), F32),
                        pltpu.VMEM((2 * n_pairs, LANES, LANES), F32)],
        compiler_params=_cp(("parallel", "arbitrary")),
    )(rw, rw, rw, rw, *params)


def _memkv_kernel(mem_ref, g_ref, w_ref, mk_ref, mv_ref, *, mem_w):
    x = mem_ref[0]
    ms = jnp.mean(x * x, axis=-1, keepdims=True)
    h = (x * lax.rsqrt(ms + RMS_EPS) * g_ref[...]).astype(BF16)
    kv = _dot(h, w_ref[...])
    mk_ref[0] = kv[:, :mem_w].astype(BF16)
    mv_ref[0] = kv[:, mem_w:].astype(BF16)


def _memkv(mem, g, w_bf, mem_w):
    b, n_mem, d = mem.shape
    out = jax.ShapeDtypeStruct((b, n_mem, mem_w), BF16)
    return pl.pallas_call(
        functools.partial(_memkv_kernel, mem_w=mem_w),
        grid=(b,),
        in_specs=[pl.BlockSpec((1, n_mem, d), lambda i: (i, 0, 0)), pl.BlockSpec((1, d), lambda i: (0, 0)),
                  pl.BlockSpec(w_bf.shape, lambda i: (0, 0))],
        out_specs=[pl.BlockSpec((1, n_mem, mem_w), lambda i: (i, 0, 0))] * 2,
        out_shape=[out, out],
        compiler_params=_cp(("parallel",)),
    )(mem, g, w_bf)


def _pack_bf16_pairs(h):
    half = h.shape[-1] // 2
    lo = pltpu.bitcast(h[:, :half].astype(BF16).astype(F32), U32)
    hi = pltpu.bitcast(h[:, half:].astype(BF16).astype(F32), U32)
    return (hi & jnp.uint32(0xFFFF0000)) | (lo >> 16)


def _unpack_bf16_pairs(p):
    lo = pltpu.bitcast(p << 16, F32)
    hi = pltpu.bitcast(p & jnp.uint32(0xFFFF0000), F32)
    return jnp.concatenate([lo, hi], axis=-1).astype(BF16)


def _merge_kernel(x_ref, yna_ref, y0_ref, y1_ref, b0_ref, b1_ref, latg_ref, memq_ref, mk_ref, mv_ref, gates_ref,
                  lng_ref, lnb_ref, g2_ref, bdm_ref, wna_ref, wrw_ref, wmem_ref, wout_ref, gffn_ref, wrhi_ref,
                  wrlo_ref,
                  x1_ref, h2_ref, aff_ref, *, d_model, mem_pairs):
    tm = x_ref.shape[1]
    ysum = y0_ref[0] + y1_ref[0]
    bdm = bdm_ref[...]
    mean = _dot_split_lhs(ysum, bdm)
    cen = ysum - mean
    var = _dot_split_lhs(cen * cen, bdm)
    y = cen * lax.rsqrt(var + GN_EPS) * lng_ref[...] + lnb_ref[...] + (b0_ref[0] + b1_ref[0])
    y_rw = y * _dot(_sigmoid(latg_ref[0]).astype(BF16), g2_ref[...])
    lane = _iota((tm, LANES), 1)
    lo_mask = lane < HEAD_DIM
    mq = memq_ref[0]
    mk = mk_ref[0]
    mv = mv_ref[0]
    outs = []
    for p in range(mem_pairs):
        sl = slice(LANES * p, LANES * (p + 1))
        qp = mq[:, sl] * (HEAD_DIM ** -0.5)
        o_h = []
        for hh in range(2):
            qm = jnp.where(lo_mask if hh == 0 else ~lo_mask, qp, jnp.zeros_like(qp))
            s = _dot_nt(qm, mk[:, sl])
            m = jnp.max(s, axis=-1, keepdims=True)
            e = jnp.exp(s - m)
            l = jnp.sum(e, axis=-1, keepdims=True)
            o_h.append(_dot(e.astype(BF16), mv[:, sl]) / l)
        outs.append(jnp.where(lo_mask, o_h[0], o_h[1]))
    y_mem = jnp.concatenate(outs, axis=-1)
    gates = gates_ref[0]
    merged = (gates[:, :d_model] * _dot(yna_ref[0], wna_ref[...])
              + gates[:, d_model:2 * d_model] * _dot(y_rw.astype(BF16), wrw_ref[...])
              + gates[:, 2 * d_model:] * _dot(y_mem.astype(BF16), wmem_ref[...]))
    x1 = x_ref[0] + _dot(merged.astype(BF16), wout_ref[...])
    x1_ref[0] = x1
    ms = jnp.mean(x1 * x1, axis=-1, keepdims=True)
    h2 = x1 * lax.rsqrt(ms + RMS_EPS) * gffn_ref[...]
    h2_ref[0] = _pack_bf16_pairs(h2)
    n_exp = aff_ref.shape[1]
    h_hi, h_lo = _split2(h2)
    hh = _dot(jnp.concatenate([h_hi, h_lo], axis=0), wrhi_ref[...])
    logits = hh[:tm] + hh[tm:] + _dot(h_hi, wrlo_ref[...])
    logits = jnp.where(lane < n_exp, logits, NEG_BIG)
    e = jnp.exp(logits - jnp.max(logits, axis=-1, keepdims=True))
    aff = e / jnp.sum(e, axis=-1, keepdims=True)
    aff_ref[0] = aff.T[:n_exp]


def _merge(x, yna, y0, y1, b0, b1, rw, memq, mk, mv, gates, ln_g, ln_b, g2, wna, wrw, wmem, wout, gffn, w_router,
           tm, rw_w):
    b, s, d = x.shape
    n_exp = w_router.shape[1]
    mem_w = memq.shape[-1]
    latg_w = g2.shape[0]
    assert latg_w == LANES and (rw.shape[-1] - latg_w) % LANES == 0 and n_exp <= LANES
    latg_blk = (rw.shape[-1] - latg_w) // LANES
    bdm = (_head_block_diag(rw_w) / HEAD_DIM).astype(BF16)
    wr = jnp.zeros((d, LANES), F32).at[:, :n_exp].set(w_router)
    wr_hi, wr_lo = _split2(wr)
    tok = lambda w: pl.BlockSpec((1, tm, w), lambda bi, j: (bi, j, 0))
    full = lambda a: pl.BlockSpec(a.shape, lambda bi, j: (0,) * a.ndim)
    per_b = lambda a: pl.BlockSpec((1,) + a.shape[1:], lambda bi, j: (bi, 0, 0))
    weights = [ln_g.reshape(1, rw_w), ln_b.reshape(1, rw_w), g2.astype(BF16), bdm, wna, wrw, wmem, wout, gffn,
               wr_hi, wr_lo]
    return pl.pallas_call(
        functools.partial(_merge_kernel, d_model=d, mem_pairs=mem_w // LANES),
        grid=(b, s // tm),
        in_specs=[tok(d), tok(rw_w), tok(rw_w), tok(rw_w), tok(rw_w), tok(rw_w),
                  pl.BlockSpec((1, tm, latg_w), lambda bi, j: (bi, j, latg_blk)),
                  tok(mem_w), per_b(mk), per_b(mv), tok(3 * d)] + [full(a) for a in weights],
        out_specs=[tok(d), tok(d // 2), pl.BlockSpec((1, n_exp, tm), lambda bi, j: (bi, 0, j))],
        out_shape=[jax.ShapeDtypeStruct((b, s, d), F32), jax.ShapeDtypeStruct((b, s, d // 2), U32),
                   jax.ShapeDtypeStruct((b, n_exp, s), F32)],
        compiler_params=_cp(("parallel", "parallel")),
    )(x, yna, y0, y1, b0, b1, rw, memq, mk, mv, gates, *weights)


def _flat_cumsum(m, upper, lower_strict):
    cr = _dot(m.astype(BF16), upper)
    rowtot = jnp.broadcast_to(cr[:, LANES - 1:LANES], cr.shape)
    rowstart = _dot(lower_strict, rowtot.astype(BF16))
    return cr, rowtot, rowstart


def _topk_kernel(aff_ref, idx_ref, val_ref, thr_ref, *, cap):
    n_exp, rows = aff_ref.shape[1], aff_ref.shape[2]
    bits_all = pltpu.bitcast(aff_ref[0], I32)

    def bisect(_, lohi):
        lo, hi = lohi
        mid = lo + ((hi - lo + 1) >> 1)
        cnt = jnp.sum(jnp.sum((bits_all >= mid).astype(I32), axis=2, keepdims=True), axis=1, keepdims=True)
        ok = cnt >= cap
        return jnp.where(ok, mid, lo), jnp.where(ok, hi, mid - 1)

    thr_all, _ = lax.fori_loop(0, 31, bisect, (jnp.zeros((n_exp, 1, 1), I32),
                                               jnp.full((n_exp, 1, 1), 0x7F800000, I32)))
    thr_ref[...] = jnp.broadcast_to(thr_all, thr_ref.shape)

    upper = (_iota((LANES, LANES), 0) <= _iota((LANES, LANES), 1)).astype(BF16)
    lower_strict = (_iota((rows, rows), 1) < _iota((rows, rows), 0)).astype(BF16)
    lower_incl = (_iota((rows, rows), 1) <= _iota((rows, rows), 0)).astype(BF16)
    p_row = _iota((1, cap), 1).astype(F32)

    def per_expert(e, carry):
        aff = aff_ref[0, e]
        bits = pltpu.bitcast(aff, I32)
        thr = thr_ref[e][0:1, :]
        gt = bits > thr
        tie = bits == thr
        need = (cap - jnp.sum(gt.astype(I32))).astype(F32)
        tie_f = tie.astype(F32)
        cr, _, rowstart = _flat_cumsum(tie_f, upper, lower_strict)
        tie_rank = rowstart + cr - tie_f
        sel = (gt | (tie & (tie_rank < need))).astype(F32)

        cr, rowtot, rowstart = _flat_cumsum(sel, upper, lower_strict)
        rowend = _dot(lower_incl, rowtot.astype(BF16))[:, 0:1]
        r_of_p = jnp.sum((rowend <= p_row).astype(I32), axis=0, keepdims=True)
        onehot = (_iota((rows, cap), 0) == r_of_p).astype(BF16)
        pieces = (cr.astype(BF16),) + _split2(rowstart) + _split3(aff)
        g = _dot_tn(jnp.concatenate(pieces, axis=1), onehot)
        blk = lambda n: g[LANES * n:LANES * (n + 1), :]
        q = p_row - (blk(1) + blk(2))[0:1, :]
        jloc = jnp.sum((blk(0) <= q).astype(I32), axis=0, keepdims=True)
        idx_ref[0, pl.ds(e, 1), :] = r_of_p * LANES + jloc
        val_ref[0, pl.ds(e, 1), :] = jnp.sum(
            jnp.where(_iota((LANES, cap), 0) == jloc, blk(3) + blk(4) + blk(5), 0.0), axis=0, keepdims=True)
        return carry

    lax.fori_loop(0, n_exp, per_expert, 0)


def _topk(aff_t, cap):
    b, n_exp, s = aff_t.shape
    assert s % LANES == 0
    rows = s // LANES
    a4 = aff_t.reshape(b, n_exp, rows, LANES)
    spec = pl.BlockSpec((1, n_exp, cap), lambda bi: (bi, 0, 0))
    return pl.pallas_call(
        functools.partial(_topk_kernel, cap=cap),
        grid=(b,),
        in_specs=[pl.BlockSpec((1, n_exp, rows, LANES), lambda bi: (bi, 0, 0, 0))],
        out_specs=[spec, spec],
        out_shape=[jax.ShapeDtypeStruct((b, n_exp, cap), I32), jax.ShapeDtypeStruct((b, n_exp, cap), F32)],
        scratch_shapes=[pltpu.VMEM((n_exp, 8, LANES), I32)],
        compiler_params=_cp(("parallel",)),
    )(a4)


def _gather_kernel(idx_ref, h_ref, o_ref, *, cap, n_exp):
    base = (pl.program_id(0) * n_exp + pl.program_id(1)) * cap

    def body(p, carry):
        i = idx_ref[base + p]
        o_ref[0, 0, pl.ds(p, 1), :] = h_ref[0, pl.ds(i, 1), :]
        return carry

    lax.fori_loop(0, cap, body, 0, unroll=8)


def _gather(idx_flat, h2p, n_exp, cap):
    b, s, w = h2p.shape
    return pl.pallas_call(
        functools.partial(_gather_kernel, cap=cap, n_exp=n_exp),
        grid_spec=pltpu.PrefetchScalarGridSpec(
            num_scalar_prefetch=1, grid=(b, n_exp),
            in_specs=[pl.BlockSpec((1, s, w), lambda bi, e, idx: (bi, 0, 0))],
            out_specs=pl.BlockSpec((1, 1, cap, w), lambda bi, e, idx: (bi, e, 0, 0))),
        out_shape=jax.ShapeDtypeStruct((b, n_exp, cap, w), U32),
        compiler_params=_cp(("parallel", "arbitrary")),
    )(idx_flat, h2p)


def _ffn_kernel(xe_ref, wg_ref, wu_ref, wd_ref, val_ref, o_ref, acc_ref):
    f = pl.program_id(2)

    @pl.when(f == 0)
    def _():
        acc_ref[...] = jnp.zeros_like(acc_ref)

    xb = _unpack_bf16_pairs(xe_ref[0, 0])
    g = _dot(xb, wg_ref[0].astype(BF16))
    u = _dot(xb, wu_ref[0].astype(BF16))
    act = (g * _sigmoid(g) * u).astype(BF16)
    acc_ref[...] += _dot(act, wd_ref[0].astype(BF16))

    @pl.when(f == pl.num_programs(2) - 1)
    def _():
        o_ref[0, 0] = acc_ref[...] * val_ref[0, 0]


def _ffn(xe, w_gate, w_up, w_down, val, fch):
    b, n_exp, cap, half = xe.shape
    d = 2 * half
    ff = w_gate.shape[-1]
    assert ff % fch == 0
    return pl.pallas_call(
        _ffn_kernel,
        grid=(n_exp, b, ff // fch),
        in_specs=[pl.BlockSpec((1, 1, cap, half), lambda e, bi, f: (bi, e, 0, 0)),
                  pl.BlockSpec((1, d, fch), lambda e, bi, f: (e, 0, f)),
                  pl.BlockSpec((1, d, fch), lambda e, bi, f: (e, 0, f)),
                  pl.BlockSpec((1, fch, d), lambda e, bi, f: (e, f, 0)),
                  pl.BlockSpec((1, 1, cap, 1), lambda e, bi, f: (bi, e, 0, 0))],
        out_specs=pl.BlockSpec((1, 1, cap, d), lambda e, bi, f: (bi, e, 0, 0)),
        out_shape=jax.ShapeDtypeStruct((b, n_exp, cap, d), F32),
        scratch_shapes=[pltpu.VMEM((cap, d), F32)],
        compiler_params=_cp(("parallel", "parallel", "arbitrary")),
    )(xe, w_gate, w_up, w_down, val)


def _combine_kernel(idx_ref, ye_ref, o_ref, *, cap, n_exp):
    e = pl.program_id(2)

    @pl.when(e == 0)
    def _():
        o_ref[...] = jnp.zeros_like(o_ref)

    base = (pl.program_id(0) * n_exp + e) * cap

    def body(g, carry):
        p0 = pl.multiple_of(g * SCATTER_GROUP, SCATTER_GROUP)
        rows = [idx_ref[base + p0 + u] for u in range(SCATTER_GROUP)]
        new = [o_ref[0, pl.ds(i, 1), :] + ye_ref[0, 0, pl.ds(p0 + u, 1), :] for u, i in enumerate(rows)]
        for i, row in zip(rows, new):
            o_ref[0, pl.ds(i, 1), :] = row
        return carry

    lax.fori_loop(0, cap // SCATTER_GROUP, body, 0)


def _combine(idx_flat, ye, s, n_split):
    b, n_exp, cap, d = ye.shape
    w = d // n_split
    return pl.pallas_call(
        functools.partial(_combine_kernel, cap=cap, n_exp=n_exp),
        grid_spec=pltpu.PrefetchScalarGridSpec(
            num_scalar_prefetch=1, grid=(b, n_split, n_exp),
            in_specs=[pl.BlockSpec((1, 1, cap, w), lambda bi, h, e, idx: (bi, e, 0, h))],
            out_specs=pl.BlockSpec((1, s, w), lambda bi, h, e, idx: (bi, 0, h))),
        out_shape=jax.ShapeDtypeStruct((b, s, d), F32),
        compiler_params=_cp(("parallel", "parallel", "arbitrary")),
    )(idx_flat, ye)


def _final_kernel(x1_ref, moe_ref, g_ref, o_ref):
    x = x1_ref[...] + moe_ref[...]
    ms = jnp.mean(x * x, axis=-1, keepdims=True)
    o_ref[...] = x * lax.rsqrt(ms + RMS_EPS) * g_ref[...]


def _final(x1, moe, g, tm):
    n, d = x1.shape
    row = pl.BlockSpec((tm, d), lambda i: (i, 0))
    return pl.pallas_call(
        _final_kernel,
        grid=(n // tm,),
        in_specs=[row, row, pl.BlockSpec((1, d), lambda i: (0, 0))],
        out_specs=row,
        out_shape=jax.ShapeDtypeStruct((n, d), F32),
        compiler_params=_cp(("parallel",)),
    )(x1, moe, g)


def _layer(x, mem, p):
    b, s, d = x.shape
    na_w = p["w_branch_na"].shape[0]
    rw_w = p["w_branch_rw"].shape[0]
    mem_w = p["w_branch_mem"].shape[0]
    n_exp = p["w_router"].shape[1]
    cap = EC_CAPACITY * s // n_exp
    rw_cols = p["w_in"].shape[1] - 3 * na_w - mem_w - 3 * d
    tm = min(256, s)

    q, k, v, rw, memq, gates = _inproj(x.reshape(b * s, d), p["norm_mix_g"].reshape(1, d), p["w_in"].astype(BF16),
                                       na_w, rw_cols, mem_w, d, tm)
    shp = lambda a: a.reshape(b, s, a.shape[-1])
    q, k, v, rw, memq, gates = map(shp, (q, k, v, rw, memq, gates))
    y_na = _na_attention(q, k, v, _na_bias_table(p["na_rpb"]), tr=min(8, s // GRID_W))
    y0, y1, b0, b1 = _rwkv(rw, p["rw_mu_rkv"], p["rw_mu_w"], p["rw_mu_a"], p["rw_w0"], p["rw_w2"], p["rw_a0"],
                           p["rw_a2"], p["rw_k_k"], p["rw_k_a"], p["rw_r_k"], rw_w)
    mk, mv = _memkv(mem, p["norm_mem_g"].reshape(1, d), p["w_mem_kv"].astype(BF16), mem_w)
    x1, h2p, aff_t = _merge(x, y_na, y0, y1, b0, b1, rw, memq, mk, mv, gates, p["rw_ln_g"], p["rw_ln_b"],
                            p["rw_g2"], p["w_branch_na"].astype(BF16), p["w_branch_rw"].astype(BF16),
                            p["w_branch_mem"].astype(BF16), p["w_out"].astype(BF16),
                            p["norm_ffn_g"].reshape(1, d), p["w_router"], tm, rw_w)
    idx, val = _topk(aff_t, cap)
    idx_flat = idx.reshape(b * n_exp * cap)
    xe = _gather(idx_flat, h2p, n_exp, cap)
    ye = _ffn(xe, p["w_exp_gate"], p["w_exp_up"], p["w_exp_down"], val.reshape(b, n_exp, cap, 1), fch=min(512, p["w_exp_gate"].shape[-1]))
    moe = _combine(idx_flat, ye, s, n_split=2)
    return x1, moe


def kernel(x, mem, norm_mix_g, norm_mem_g, w_in, na_rpb, rw_mu_rkv, rw_mu_w, rw_mu_a, rw_w0, rw_w2, rw_a0, rw_a2,
           rw_k_k, rw_k_a, rw_r_k, rw_g2, rw_ln_g, rw_ln_b, w_mem_kv, w_branch_na, w_branch_rw, w_branch_mem, w_out,
           norm_ffn_g, w_router, w_exp_gate, w_exp_up, w_exp_down, norm_final_g):
    stacked = dict(norm_mix_g=norm_mix_g, norm_mem_g=norm_mem_g, w_in=w_in, na_rpb=na_rpb, rw_mu_rkv=rw_mu_rkv,
                   rw_mu_w=rw_mu_w, rw_mu_a=rw_mu_a, rw_w0=rw_w0, rw_w2=rw_w2, rw_a0=rw_a0, rw_a2=rw_a2,
                   rw_k_k=rw_k_k, rw_k_a=rw_k_a, rw_r_k=rw_r_k, rw_g2=rw_g2, rw_ln_g=rw_ln_g, rw_ln_b=rw_ln_b,
                   w_mem_kv=w_mem_kv, w_branch_na=w_branch_na, w_branch_rw=w_branch_rw, w_branch_mem=w_branch_mem,
                   w_out=w_out, norm_ffn_g=norm_ffn_g, w_router=w_router, w_exp_gate=w_exp_gate,
                   w_exp_up=w_exp_up, w_exp_down=w_exp_down)
    b, s, d = x.shape
    depth = w_in.shape[0]
    tm = min(256, s)
    g_one = jnp.ones((1, d), F32)
    for l in range(depth):
        p = {name: a[l] for name, a in stacked.items()}
        x1, moe = _layer(x, mem, p)
        last = l == depth - 1
        assert last, "stacks deeper than one layer are not supported"
        x = _final(x1.reshape(b * s, d), moe.reshape(b * s, d), norm_final_g.reshape(1, d) if last else g_one,
                   tm).reshape(b, s, d)
    return x
```

```python
import functools

import jax
import jax.numpy as jnp
from jax import lax
from jax.experimental import pallas as pl
from jax.experimental.pallas import tpu as pltpu

F32 = jnp.float32
BF16 = jnp.bfloat16
I32 = jnp.int32
U32 = jnp.uint32
HI = lax.Precision.HIGHEST

HEAD_DIM = 64
GRID_W = 64
NA_WIN_ROWS = 8
NA_WIN_COLS = 16
EC_CAPACITY = 2
RMS_EPS = 1e-6
GN_EPS = 64e-5
NEG_BIG = -1e30

LANES = 128
CHUNK = 64
SCATTER_GROUP = 8
VMEM_LIMIT = 56 * 1024 * 1024


def _cp(sem, vmem=VMEM_LIMIT):
    return pltpu.CompilerParams(dimension_semantics=sem, vmem_limit_bytes=vmem)


def _dot(a, b, prec=None):
    return jnp.dot(a, b, preferred_element_type=F32, precision=prec)


def _dot_nt(a, b, prec=None):
    return lax.dot_general(a, b, (((1,), (1,)), ((), ())), preferred_element_type=F32, precision=prec)


def _dot_tn(a, b, prec=None):
    return lax.dot_general(a, b, (((0,), (0,)), ((), ())), preferred_element_type=F32, precision=prec)


def _sigmoid(x):
    return 1.0 / (1.0 + jnp.exp(-x))


def _split2(x):
    hi = x.astype(BF16)
    return hi, (x - hi.astype(F32)).astype(BF16)


def _split3(x):
    hi, rest = x.astype(BF16), x - x.astype(BF16).astype(F32)
    mid = rest.astype(BF16)
    return hi, mid, (rest - mid.astype(F32)).astype(BF16)


def _dot_split_lhs(x, w_bf):
    n = x.shape[0]
    r = _dot(jnp.concatenate(_split2(x), axis=0), w_bf)
    return r[:n] + r[n:]


def _iota(shape, dim):
    return lax.broadcasted_iota(I32, shape, dim)


def _inproj_kernel(x_ref, g_ref, w_ref, q_ref, k_ref, v_ref, rw_ref, memq_ref, gates_ref, *, cuts):
    x = x_ref[...]
    ms = jnp.mean(x * x, axis=-1, keepdims=True)
    h = (x * lax.rsqrt(ms + RMS_EPS) * g_ref[...]).astype(BF16)
    c = cuts
    q_ref[...] = _dot(h, w_ref[:, c[0]:c[1]]).astype(BF16)
    k_ref[...] = _dot(h, w_ref[:, c[1]:c[2]]).astype(BF16)
    v_ref[...] = _dot(h, w_ref[:, c[2]:c[3]]).astype(BF16)
    rw_ref[...] = _dot(h, w_ref[:, c[3]:c[4]])
    memq_ref[...] = _dot(h, w_ref[:, c[4]:c[5]]).astype(BF16)
    gates_ref[...] = _sigmoid(_dot(h, w_ref[:, c[5]:c[6]])).astype(BF16)


def _inproj(x2, g, w_bf, na_w, rw_w, mem_w, d_model, tm):
    n = x2.shape[0]
    d_in = w_bf.shape[1]
    cuts = (0, na_w, 2 * na_w, 3 * na_w, 3 * na_w + rw_w, 3 * na_w + rw_w + mem_w, d_in)
    assert cuts[6] - cuts[5] == 3 * d_model
    row = lambda w: pl.BlockSpec((tm, w), lambda i: (i, 0))
    return pl.pallas_call(
        functools.partial(_inproj_kernel, cuts=cuts),
        grid=(n // tm,),
        in_specs=[row(d_model), pl.BlockSpec((1, d_model), lambda i: (0, 0)),
                  pl.BlockSpec((d_model, d_in), lambda i: (0, 0))],
        out_specs=[row(na_w), row(na_w), row(na_w), row(rw_w), row(mem_w), row(3 * d_model)],
        out_shape=[jax.ShapeDtypeStruct((n, na_w), BF16)] * 3
        + [jax.ShapeDtypeStruct((n, rw_w), F32), jax.ShapeDtypeStruct((n, mem_w), BF16),
           jax.ShapeDtypeStruct((n, 3 * d_model), BF16)],
        compiler_params=_cp(("parallel",)),
    )(x2, g, w_bf)


def _na_bias_kernel(rpb_ref, o_ref):
    w, kc = GRID_W, NA_WIN_COLS
    shape = (rpb_ref.shape[1], w * w)
    o = _iota(shape, 0)
    flat = _iota(shape, 1)
    qc = flat // w
    c = flat % w
    cs = jnp.clip(qc - kc // 2, 0, w - kc)
    valid = (c >= cs) & (c < cs + kc)
    onehot = ((c - qc + (kc - 1) == o) & valid).astype(F32)
    o_ref[...] = _dot(rpb_ref[...], onehot, HI) + jnp.where(valid[0:1], 0.0, NEG_BIG)


def _na_bias_table(rpb):
    kr, w = NA_WIN_ROWS, GRID_W
    h, n_ro, n_co = rpb.shape
    rows = -(-h * n_ro // 8) * 8
    rpb2 = jnp.zeros((rows, LANES), F32).at[:h * n_ro, :n_co].set(rpb.reshape(h * n_ro, n_co).astype(F32))
    tab = pl.pallas_call(
        _na_bias_kernel,
        out_shape=jax.ShapeDtypeStruct((rows, w * w), F32),
    )(rpb2)[:h * n_ro].reshape(h, n_ro, w, w)
    t = jnp.stack([tab[:, kr - 1 - dl:2 * kr - 1 - dl] for dl in range(kr)], axis=1)
    return t.transpose(0, 1, 3, 2, 4).reshape(h, kr, w, kr * w)


def _na_kernel(q_ref, k_ref, v_ref, bias_ref, o_ref, *, tr, rows, n_pairs):
    j = pl.program_id(1)
    w = GRID_W
    slab = NA_WIN_ROWS * w
    lane = _iota((w, LANES), 1)
    lo_mask = lane < HEAD_DIM

    def row(i, carry):
        r = j * tr + i
        rs = jnp.clip(r - NA_WIN_ROWS // 2, 0, rows - NA_WIN_ROWS)
        delta = r - rs
        q = q_ref[0, pl.ds(pl.multiple_of(i * w, w), w), :] * (HEAD_DIM ** -0.5)
        k0 = pl.multiple_of(rs * w, w)
        kk = k_ref[0, pl.ds(k0, slab), :]
        vv = v_ref[0, pl.ds(k0, slab), :]
        pair = lambda t, p: t[:, LANES * p:LANES * (p + 1)]
        qm = [_stack_pair(pair(q, p), lo_mask) for p in range(n_pairs)]
        s = [_dot_nt(qm[p], pair(kk, p))
             + jnp.concatenate([bias_ref[2 * p, delta], bias_ref[2 * p + 1, delta]], axis=0) for p in range(n_pairs)]
        e = [jnp.exp(si - jnp.max(si, axis=-1, keepdims=True)) for si in s]
        o = [_dot(e[p].astype(BF16), pair(vv, p)) / jnp.sum(e[p], axis=-1, keepdims=True) for p in range(n_pairs)]
        outs = [jnp.where(lo_mask, oi[:w], oi[w:]) for oi in o]
        o_ref[0, pl.ds(pl.multiple_of(i * w, w), w), :] = jnp.concatenate(outs, axis=-1).astype(o_ref.dtype)
        return carry

    lax.fori_loop(0, tr, row, 0, unroll=2)


def _na_attention(q, k, v, bias, tr):
    b, s, c = q.shape
    rows = s // GRID_W
    n_heads = c // HEAD_DIM
    assert n_heads % 2 == 0 and rows >= NA_WIN_ROWS and rows % tr == 0
    return pl.pallas_call(
        functools.partial(_na_kernel, tr=tr, rows=rows, n_pairs=n_heads // 2),
        grid=(b, rows // tr),
        in_specs=[pl.BlockSpec((1, tr * GRID_W, c), lambda bi, j: (bi, j, 0)),
                  pl.BlockSpec((1, s, c), lambda bi, j: (bi, 0, 0)),
                  pl.BlockSpec((1, s, c), lambda bi, j: (bi, 0, 0)),
                  pl.BlockSpec(bias.shape, lambda bi, j: (0, 0, 0, 0))],
        out_specs=pl.BlockSpec((1, tr * GRID_W, c), lambda bi, j: (bi, j, 0)),
        out_shape=jax.ShapeDtypeStruct((b, s, c), BF16),
        compiler_params=_cp(("parallel", "arbitrary")),
    )(q, k, v, bias)


def _softplus(z):
    return jnp.maximum(z, 0.0) + jnp.log(1.0 + jnp.exp(-jnp.abs(z)))


def _stack_pair(xp, lo_mask):
    zero = jnp.zeros_like(xp)
    return jnp.concatenate([jnp.where(lo_mask, xp, zero), jnp.where(lo_mask, zero, xp)], axis=0)


def _rwkv_prep(d, x, edge, prm, rw_w):
    (mu_rkv, mu_wa, w0, wl_hi, a0, wl_lo, k_k, k_a, r_k, bd) = prm
    L = CHUNK
    rowi = _iota((L, 1), 0)
    if d == 0:
        xs = jnp.where(rowi == 0, edge, pltpu.roll(x, 1, 0))
    else:
        xs = jnp.where(rowi == L - 1, edge, pltpu.roll(x, L - 1, 0))

    def mix(lo, hi, mu):
        t = x[:, lo:hi]
        return t + (xs[:, lo:hi] - t) * mu

    r = mix(0, rw_w, mu_rkv[d, 0:1])
    k = mix(rw_w, 2 * rw_w, mu_rkv[d, 1:2])
    v = mix(2 * rw_w, 3 * rw_w, mu_rkv[d, 2:3])
    lwa = mix(3 * rw_w, 3 * rw_w + LANES, mu_wa[d:d + 1])
    lane = _iota((L, LANES), 1)
    tw = jnp.where(lane < HEAD_DIM, jnp.tanh(lwa), lwa)
    tw_hi, tw_lo = _split2(tw)
    hh = _dot(jnp.concatenate([tw_hi, tw_lo], axis=0), wl_hi[d])
    lora = hh[:L] + hh[L:] + _dot(tw_hi, wl_lo[d])
    yield
    w_log = -_softplus(-(w0[d:d + 1] + lora[:, :rw_w])) - 0.5
    logw = -jnp.exp(w_log)
    a = _sigmoid(a0[d:d + 1] + lora[:, rw_w:])
    kk = k * k_k
    k = k * (1.0 + (a - 1.0) * k_a)
    sums = _dot(jnp.concatenate(_split2(kk * kk) + _split2(r * k * r_k), axis=0), bd)
    ti = _iota((L, L), 0)
    si = _iota((L, L), 1)
    tri = ((si <= ti) if d == 0 else (si >= ti)).astype(BF16)
    lc3 = _dot(tri, jnp.concatenate(_split3(logw), axis=1))
    yield
    kk = kk / jnp.maximum(jnp.sqrt(sums[:L] + sums[L:2 * L]), 1e-12)
    bonus = (sums[2 * L:3 * L] + sums[3 * L:]) * v
    bv = kk * a
    lc = lc3[:, :rw_w] + lc3[:, rw_w:2 * rw_w] + lc3[:, 2 * rw_w:]
    tot = jnp.sum(logw, axis=0, keepdims=True)
    e_neg = jnp.exp(-lc)
    e_end = jnp.exp(tot - lc)
    ops = dict(r=r * jnp.exp(lc), a=-kk * jnp.exp(lc - logw), b=bv * e_neg, k=k * e_neg, v=v, bh=bv * e_end,
               kh=k * e_end)
    return ops, jnp.exp(tot), bonus


RWKV_OPERANDS = ("a", "r", "b", "k", "v", "bh", "kh")


def _advance(gens, segments):
    for _ in range(segments):
        for g in gens:
            next(g)


def _finish(gen):
    try:
        while True:
            next(gen)
    except StopIteration as stop:
        return stop.value


def _rwkv_stack(preps, n_pairs):
    lo_mask = _iota((CHUNK, LANES), 1) < HEAD_DIM
    inst = [(d, p) for d in range(len(preps)) for p in range(n_pairs)]
    ops = [[_stack_pair(preps[d][0][name][:, LANES * p:LANES * (p + 1)].astype(BF16), lo_mask) for d, p in inst]
           for name in RWKV_OPERANDS]
    gam = [preps[d][1][:, LANES * p:LANES * (p + 1)] for d, p in inst]
    return ops, gam


def _rwkv_blocks(ops, gam, state_ref, n_dirs, n_pairs):
    L = CHUNK
    n2 = 2 * L
    ii = _iota((n2, n2), 0)
    jj = _iota((n2, n2), 1)
    strict = ((jj < ii), (jj > ii))
    incl = ((jj <= ii), (jj >= ii))
    eye = (ii == jj).astype(F32)
    zero = jnp.zeros((n2, n2), F32)
    inst = [(d, p) for d in range(n_dirs) for p in range(n_pairs)]
    each = lambda f, *ls: [f(*a) for a in zip(*ls)]
    a_s, r_s, b_s, k_s, v_s, bh_s, kh_s = ops
    g = each(lambda a, r, b, k: _dot_nt(jnp.concatenate([a, r], axis=0), jnp.concatenate([b, k], axis=0)),
             a_s, r_s, b_s, k_s)
    n_ab = [jnp.where(strict[d], gi[:n2, :n2], zero) for (d, _), gi in zip(inst, g)]
    a_ak = [jnp.where(strict[d], gi[:n2, n2:], zero).astype(BF16) for (d, _), gi in zip(inst, g)]
    a_rb = [jnp.where(incl[d], gi[n2:, :n2], zero).astype(BF16) for (d, _), gi in zip(inst, g)]
    a_rk = [jnp.where(incl[d], gi[n2:, n2:], zero).astype(BF16) for (d, _), gi in zip(inst, g)]
    yield
    av = each(lambda ak, rk, v: _dot(jnp.concatenate([ak, rk], axis=0), v), a_ak, a_rk, v_s)
    t_inv = [eye + n for n in n_ab]
    npow = each(lambda n: _dot(n, n), [n.astype(BF16) for n in n_ab])
    yield
    for _ in range(CHUNK.bit_length() - 3):
        nb = [n.astype(BF16) for n in npow]
        both = each(lambda t, n: _dot(jnp.concatenate([t.astype(BF16), n], axis=0), n), t_inv, nb)
        t_inv = each(lambda t, x: t + x[:n2], t_inv, both)
        npow = [x[n2:] for x in both]
        yield
    t_inv = each(lambda t, n: t + _dot(t.astype(BF16), n.astype(BF16)), t_inv, npow)
    pq = each(lambda t, a, x: _dot(t.astype(BF16), jnp.concatenate([a, x[:n2].astype(BF16)], axis=1)).astype(BF16),
              t_inv, a_s, av)
    yield
    arb_pq = each(_dot, a_rb, pq)
    ry = each(lambda r, x: (r.astype(F32) + x[:, :LANES]).astype(BF16), r_s, arb_pq)
    y0 = each(lambda x, a: x[:, LANES:] + a[n2:], arb_pq, av)
    yield
    pq_bh = each(_dot_tn, pq, bh_s)
    m_mat = each(lambda gm, x: (eye * gm + x[:LANES]).astype(BF16), gam, pq_bh)
    c_mat = each(lambda x, v, kh: x[LANES:] + _dot_tn(v, kh), pq_bh, v_s, kh_s)
    yield
    s_old = [state_ref[i].astype(BF16) for i in range(len(inst))]
    y = each(lambda r, s, y_: _dot_nt(r, s) + y_, ry, s_old, y0)
    s_new = each(lambda s, m, c: _dot(s, m) + c, s_old, m_mat, c_mat)
    for i, s in enumerate(s_new):
        state_ref[i] = s
    return [jnp.concatenate([yi[:L] + yi[L:] for (di, _), yi in zip(inst, y) if di == d], axis=-1)
            for d in range(n_dirs)]


def _rwkv_kernel(cur0_ref, prev0_ref, cur1_ref, next1_ref, mu_rkv_ref, mu_wa_ref, w0_ref, wlhi_ref, a0_ref,
                 wllo_ref, kk_ref, ka_ref, rk_ref, bd_ref, y0_ref, y1_ref, b0_ref, b1_ref, ops_ref, gam_ref,
                 state_ref, *, n_pairs, rw_w):
    c = pl.program_id(1)

    @pl.when(c == 0)
    def _():
        ops_ref[...] = jnp.zeros_like(ops_ref)
        gam_ref[...] = jnp.zeros_like(gam_ref)
        state_ref[...] = jnp.zeros_like(state_ref)

    n_inst = 2 * n_pairs
    staged = [[ops_ref[n, i] for i in range(n_inst)] for n in range(len(RWKV_OPERANDS))]
    staged_gam = [gam_ref[i] for i in range(n_inst)]

    prm = (mu_rkv_ref[...], mu_wa_ref[...], w0_ref[...], wlhi_ref[...], a0_ref[...], wllo_ref[...],
           kk_ref[...], ka_ref[...], rk_ref[...], bd_ref[...])
    first = jnp.minimum(c, pl.num_programs(1) - 2) == 0
    edge0 = jnp.where(first, 0.0, prev0_ref[0, 7:8, :])
    edge1 = jnp.where(first, 0.0, next1_ref[0, 0:1, :])
    prep = [_rwkv_prep(0, cur0_ref[0], edge0, prm, rw_w), _rwkv_prep(1, cur1_ref[0], edge1, prm, rw_w)]
    blocks = _rwkv_blocks(staged, staged_gam, state_ref, 2, n_pairs)
    _advance([blocks], 2)
    _advance(prep, 1)
    _advance([blocks], 2)
    _advance(prep, 1)
    _advance([blocks], 3)
    preps = [_finish(g) for g in prep]
    b0_ref[0] = preps[0][2]
    b1_ref[0] = preps[1][2]
    ops, gam = _rwkv_stack(preps, n_pairs)
    for n, per_inst in enumerate(ops):
        for i, t in enumerate(per_inst):
            ops_ref[n, i] = t
    for i, t in enumerate(gam):
        gam_ref[i] = t

    ys = _finish(blocks)
    y0_ref[0] = ys[0]
    y1_ref[0] = ys[1]


def _head_block_diag(width):
    h = jnp.arange(width) // HEAD_DIM
    return (h[:, None] == h[None, :]).astype(F32)


def _rwkv(rw, mu_rkv, mu_w, mu_a, w0, w2, a0, a2, k_k, k_a, r_k, rw_w):
    b, s, cols = rw.shape
    assert s % CHUNK == 0 and rw_w % LANES == 0 and mu_w.shape[-1] == HEAD_DIM and mu_a.shape[-1] == HEAD_DIM
    nc = s // CHUNK
    n_pairs = rw_w // LANES
    sub = CHUNK // 8
    mu_wa = jnp.concatenate([mu_w, mu_a], axis=-1)
    wl = jnp.concatenate([jnp.concatenate([w2, jnp.zeros_like(w2)], axis=1),
                          jnp.concatenate([jnp.zeros_like(a2), a2], axis=1)], axis=2)
    wl_hi, wl_lo = _split2(wl)
    bd = _head_block_diag(rw_w).astype(BF16)
    full = lambda a: pl.BlockSpec(a.shape, lambda bi, c: (0,) * a.ndim)
    params = [mu_rkv, mu_wa, w0, wl_hi, a0, wl_lo, k_k.reshape(1, rw_w), k_a.reshape(1, rw_w),
              r_k.reshape(1, rw_w), bd]
    out = jax.ShapeDtypeStruct((b, s, rw_w), F32)
    cp = lambda c: jnp.minimum(c, nc - 1)
    cm = lambda c: jnp.maximum(c - 1, 0)
    blk = lambda chunk_of, rev: pl.BlockSpec(
        (1, CHUNK, rw_w), lambda bi, c: (bi, nc - 1 - chunk_of(c) if rev else chunk_of(c), 0))
    return pl.pallas_call(
        functools.partial(_rwkv_kernel, n_pairs=n_pairs, rw_w=rw_w),
        grid=(b, nc + 1),
        in_specs=[pl.BlockSpec((1, CHUNK, cols), lambda bi, c: (bi, cp(c), 0)),
                  pl.BlockSpec((1, 8, cols), lambda bi, c: (bi, jnp.maximum(cp(c) * sub - 1, 0), 0)),
                  pl.BlockSpec((1, CHUNK, cols), lambda bi, c: (bi, nc - 1 - cp(c), 0)),
                  pl.BlockSpec((1, 8, cols),
                               lambda bi, c: (bi, jnp.minimum((nc - cp(c)) * sub, nc * sub - 1), 0))]
        + [full(a) for a in params],
        out_specs=[blk(cm, False), blk(cm, True), blk(cp, False), blk(cp, True)],
        out_shape=[out] * 4,
        scratch_shapes=[pltpu.VMEM((len(RWKV_OPERANDS), 2 * n_pairs, 2 * CHUNK, LANES), BF16),
                        pltpu.VMEM((2 * n_pairs, 1, LANES), F32),
                        pltpu.VMEM((2 * n_pairs, LANES, LANES), F32)],
        compiler_params=_cp(("parallel", "arbitrary")),
    )(rw, rw, rw, rw, *params)


def _memkv_kernel(mem_ref, g_ref, w_ref, mk_ref, mv_ref, *, mem_w):
    x = mem_ref[0]
    ms = jnp.mean(x * x, axis=-1, keepdims=True)
    h = (x * lax.rsqrt(ms + RMS_EPS) * g_ref[...]).astype(BF16)
    kv = _dot(h, w_ref[...])
    mk_ref[0] = kv[:, :mem_w].astype(BF16)
    mv_ref[0] = kv[:, mem_w:].astype(BF16)


def _memkv(mem, g, w_bf, mem_w):
    b, n_mem, d = mem.shape
    out = jax.ShapeDtypeStruct((b, n_mem, mem_w), BF16)
    return pl.pallas_call(
        functools.partial(_memkv_kernel, mem_w=mem_w),
        grid=(b,),
        in_specs=[pl.BlockSpec((1, n_mem, d), lambda i: (i, 0, 0)), pl.BlockSpec((1, d), lambda i: (0, 0)),
                  pl.BlockSpec(w_bf.shape, lambda i: (0, 0))],
        out_specs=[pl.BlockSpec((1, n_mem, mem_w), lambda i: (i, 0, 0))] * 2,
        out_shape=[out, out],
        compiler_params=_cp(("parallel",)),
    )(mem, g, w_bf)


def _pack_bf16_pairs(h):
    half = h.shape[-1] // 2
    lo = pltpu.bitcast(h[:, :half].astype(BF16).astype(F32), U32)
    hi = pltpu.bitcast(h[:, half:].astype(BF16).astype(F32), U32)
    return (hi & jnp.uint32(0xFFFF0000)) | (lo >> 16)


def _unpack_bf16_pairs(p):
    lo = pltpu.bitcast(p << 16, F32)
    hi = pltpu.bitcast(p & jnp.uint32(0xFFFF0000), F32)
    return jnp.concatenate([lo, hi], axis=-1).astype(BF16)


def _merge_kernel(x_ref, yna_ref, y0_ref, y1_ref, b0_ref, b1_ref, latg_ref, memq_ref, mk_ref, mv_ref, gates_ref,
                  lng_ref, lnb_ref, g2_ref, bdm_ref, wna_ref, wrw_ref, wmem_ref, wout_ref, gffn_ref, wrhi_ref,
                  wrlo_ref,
                  x1_ref, h2_ref, aff_ref, *, d_model, mem_pairs):
    tm = x_ref.shape[1]
    ysum = y0_ref[0] + y1_ref[0]
    bdm = bdm_ref[...]
    mean = _dot_split_lhs(ysum, bdm)
    cen = ysum - mean
    var = _dot_split_lhs(cen * cen, bdm)
    y = cen * lax.rsqrt(var + GN_EPS) * lng_ref[...] + lnb_ref[...] + (b0_ref[0] + b1_ref[0])
    y_rw = y * _dot(_sigmoid(latg_ref[0]).astype(BF16), g2_ref[...])
    lane = _iota((tm, LANES), 1)
    lo_mask = lane < HEAD_DIM
    mq = memq_ref[0]
    mk = mk_ref[0]
    mv = mv_ref[0]
    outs = []
    for p in range(mem_pairs):
        sl = slice(LANES * p, LANES * (p + 1))
        qp = mq[:, sl] * (HEAD_DIM ** -0.5)
        o_h = []
        for hh in range(2):
            qm = jnp.where(lo_mask if hh == 0 else ~lo_mask, qp, jnp.zeros_like(qp))
            s = _dot_nt(qm, mk[:, sl])
            m = jnp.max(s, axis=-1, keepdims=True)
            e = jnp.exp(s - m)
            l = jnp.sum(e, axis=-1, keepdims=True)
            o_h.append(_dot(e.astype(BF16), mv[:, sl]) / l)
        outs.append(jnp.where(lo_mask, o_h[0], o_h[1]))
    y_mem = jnp.concatenate(outs, axis=-1)
    gates = gates_ref[0]
    merged = (gates[:, :d_model] * _dot(yna_ref[0], wna_ref[...])
              + gates[:, d_model:2 * d_model] * _dot(y_rw.astype(BF16), wrw_ref[...])
              + gates[:, 2 * d_model:] * _dot(y_mem.astype(BF16), wmem_ref[...]))
    x1 = x_ref[0] + _dot(merged.astype(BF16), wout_ref[...])
    x1_ref[0] = x1
    ms = jnp.mean(x1 * x1, axis=-1, keepdims=True)
    h2 = x1 * lax.rsqrt(ms + RMS_EPS) * gffn_ref[...]
    h2_ref[0] = _pack_bf16_pairs(h2)
    n_exp = aff_ref.shape[1]
    h_hi, h_lo = _split2(h2)
    hh = _dot(jnp.concatenate([h_hi, h_lo], axis=0), wrhi_ref[...])
    logits = hh[:tm] + hh[tm:] + _dot(h_hi, wrlo_ref[...])
    logits = jnp.where(lane < n_exp, logits, NEG_BIG)
    e = jnp.exp(logits - jnp.max(logits, axis=-1, keepdims=True))
    aff = e / jnp.sum(e, axis=-1, keepdims=True)
    aff_ref[0] = aff.T[:n_exp]


def _merge(x, yna, y0, y1, b0, b1, rw, memq, mk, mv, gates, ln_g, ln_b, g2, wna, wrw, wmem, wout, gffn, w_router,
           tm, rw_w):
    b, s, d = x.shape
    n_exp = w_router.shape[1]
    mem_w = memq.shape[-1]
    latg_w = g2.shape[0]
    assert latg_w == LANES and (rw.shape[-1] - latg_w) % LANES == 0 and n_exp <= LANES
    latg_blk = (rw.shape[-1] - latg_w) // LANES
    bdm = (_head_block_diag(rw_w) / HEAD_DIM).astype(BF16)
    wr = jnp.zeros((d, LANES), F32).at[:, :n_exp].set(w_router)
    wr_hi, wr_lo = _split2(wr)
    tok = lambda w: pl.BlockSpec((1, tm, w), lambda bi, j: (bi, j, 0))
    full = lambda a: pl.BlockSpec(a.shape, lambda bi, j: (0,) * a.ndim)
    per_b = lambda a: pl.BlockSpec((1,) + a.shape[1:], lambda bi, j: (bi, 0, 0))
    weights = [ln_g.reshape(1, rw_w), ln_b.reshape(1, rw_w), g2.astype(BF16), bdm, wna, wrw, wmem, wout, gffn,
               wr_hi, wr_lo]
    return pl.pallas_call(
        functools.partial(_merge_kernel, d_model=d, mem_pairs=mem_w // LANES),
        grid=(b, s // tm),
        in_specs=[tok(d), tok(rw_w), tok(rw_w), tok(rw_w), tok(rw_w), tok(rw_w),
                  pl.BlockSpec((1, tm, latg_w), lambda bi, j: (bi, j, latg_blk)),
                  tok(mem_w), per_b(mk), per_b(mv), tok(3 * d)] + [full(a) for a in weights],
        out_specs=[tok(d), tok(d // 2), pl.BlockSpec((1, n_exp, tm), lambda bi, j: (bi, 0, j))],
        out_shape=[jax.ShapeDtypeStruct((b, s, d), F32), jax.ShapeDtypeStruct((b, s, d // 2), U32),
                   jax.ShapeDtypeStruct((b, n_exp, s), F32)],
        compiler_params=_cp(("parallel", "parallel")),
    )(x, yna, y0, y1, b0, b1, rw, memq, mk, mv, gates, *weights)


def _flat_cumsum(m, upper, lower_strict):
    cr = _dot(m.astype(BF16), upper)
    rowtot = jnp.broadcast_to(cr[:, LANES - 1:LANES], cr.shape)
    rowstart = _dot(lower_strict, rowtot.astype(BF16))
    return cr, rowtot, rowstart


def _topk_kernel(aff_ref, idx_ref, val_ref, thr_ref, *, cap):
    n_exp, rows = aff_ref.shape[1], aff_ref.shape[2]
    bits_all = pltpu.bitcast(aff_ref[0], I32)

    def bisect(_, lohi):
        lo, hi = lohi
        mid = lo + ((hi - lo + 1) >> 1)
        cnt = jnp.sum(jnp.sum((bits_all >= mid).astype(I32), axis=2, keepdims=True), axis=1, keepdims=True)
        ok = cnt >= cap
        return jnp.where(ok, mid, lo), jnp.where(ok, hi, mid - 1)

    thr_all, _ = lax.fori_loop(0, 31, bisect, (jnp.zeros((n_exp, 1, 1), I32),
                                               jnp.full((n_exp, 1, 1), 0x7F800000, I32)))
    thr_ref[...] = jnp.broadcast_to(thr_all, thr_ref.shape)

    upper = (_iota((LANES, LANES), 0) <= _iota((LANES, LANES), 1)).astype(BF16)
    lower_strict = (_iota((rows, rows), 1) < _iota((rows, rows), 0)).astype(BF16)
    lower_incl = (_iota((rows, rows), 1) <= _iota((rows, rows), 0)).astype(BF16)
    p_row = _iota((1, cap), 1).astype(F32)

    def per_expert(e, carry):
        aff = aff_ref[0, e]
        bits = pltpu.bitcast(aff, I32)
        thr = thr_ref[e][0:1, :]
        gt = bits > thr
        tie = bits == thr
        need = (cap - jnp.sum(gt.astype(I32))).astype(F32)
        tie_f = tie.astype(F32)
        cr, _, rowstart = _flat_cumsum(tie_f, upper, lower_strict)
        tie_rank = rowstart + cr - tie_f
        sel = (gt | (tie & (tie_rank < need))).astype(F32)

        cr, rowtot, rowstart = _flat_cumsum(sel, upper, lower_strict)
        rowend = _dot(lower_incl, rowtot.astype(BF16))[:, 0:1]
        r_of_p = jnp.sum((rowend <= p_row).astype(I32), axis=0, keepdims=True)
        onehot = (_iota((rows, cap), 0) == r_of_p).astype(BF16)
        pieces = (cr.astype(BF16),) + _split2(rowstart) + _split3(aff)
        g = _dot_tn(jnp.concatenate(pieces, axis=1), onehot)
        blk = lambda n: g[LANES * n:LANES * (n + 1), :]
        q = p_row - (blk(1) + blk(2))[0:1, :]
        jloc = jnp.sum((blk(0) <= q).astype(I32), axis=0, keepdims=True)
        idx_ref[0, pl.ds(e, 1), :] = r_of_p * LANES + jloc
        val_ref[0, pl.ds(e, 1), :] = jnp.sum(
            jnp.where(_iota((LANES, cap), 0) == jloc, blk(3) + blk(4) + blk(5), 0.0), axis=0, keepdims=True)
        return carry

    lax.fori_loop(0, n_exp, per_expert, 0)


def _topk(aff_t, cap):
    b, n_exp, s = aff_t.shape
    assert s % LANES == 0
    rows = s // LANES
    a4 = aff_t.reshape(b, n_exp, rows, LANES)
    spec = pl.BlockSpec((1, n_exp, cap), lambda bi: (bi, 0, 0))
    return pl.pallas_call(
        functools.partial(_topk_kernel, cap=cap),
        grid=(b,),
        in_specs=[pl.BlockSpec((1, n_exp, rows, LANES), lambda bi: (bi, 0, 0, 0))],
        out_specs=[spec, spec],
        out_shape=[jax.ShapeDtypeStruct((b, n_exp, cap), I32), jax.ShapeDtypeStruct((b, n_exp, cap), F32)],
        scratch_shapes=[pltpu.VMEM((n_exp, 8, LANES), I32)],
        compiler_params=_cp(("parallel",)),
    )(a4)


def _gather_kernel(idx_ref, h_ref, o_ref, *, cap, n_exp):
    base = (pl.program_id(0) * n_exp + pl.program_id(1)) * cap

    def body(p, carry):
        i = idx_ref[base + p]
        o_ref[0, 0, pl.ds(p, 1), :] = h_ref[0, pl.ds(i, 1), :]
        return carry

    lax.fori_loop(0, cap, body, 0, unroll=8)


def _gather(idx_flat, h2p, n_exp, cap):
    b, s, w = h2p.shape
    return pl.pallas_call(
        functools.partial(_gather_kernel, cap=cap, n_exp=n_exp),
        grid_spec=pltpu.PrefetchScalarGridSpec(
            num_scalar_prefetch=1, grid=(b, n_exp),
            in_specs=[pl.BlockSpec((1, s, w), lambda bi, e, idx: (bi, 0, 0))],
            out_specs=pl.BlockSpec((1, 1, cap, w), lambda bi, e, idx: (bi, e, 0, 0))),
        out_shape=jax.ShapeDtypeStruct((b, n_exp, cap, w), U32),
        compiler_params=_cp(("parallel", "arbitrary")),
    )(idx_flat, h2p)


def _ffn_kernel(xe_ref, wg_ref, wu_ref, wd_ref, val_ref, o_ref, acc_ref):
    f = pl.program_id(2)

    @pl.when(f == 0)
    def _():
        acc_ref[...] = jnp.zeros_like(acc_ref)

    xb = _unpack_bf16_pairs(xe_ref[0, 0])
    g = _dot(xb, wg_ref[0].astype(BF16))
    u = _dot(xb, wu_ref[0].astype(BF16))
    act = (g * _sigmoid(g) * u).astype(BF16)
    acc_ref[...] += _dot(act, wd_ref[0].astype(BF16))

    @pl.when(f == pl.num_programs(2) - 1)
    def _():
        o_ref[0, 0] = acc_ref[...] * val_ref[0, 0]


def _ffn(xe, w_gate, w_up, w_down, val, fch):
    b, n_exp, cap, half = xe.shape
    d = 2 * half
    ff = w_gate.shape[-1]
    assert ff % fch == 0
    return pl.pallas_call(
        _ffn_kernel,
        grid=(n_exp, b, ff // fch),
        in_specs=[pl.BlockSpec((1, 1, cap, half), lambda e, bi, f: (bi, e, 0, 0)),
                  pl.BlockSpec((1, d, fch), lambda e, bi, f: (e, 0, f)),
                  pl.BlockSpec((1, d, fch), lambda e, bi, f: (e, 0, f)),
                  pl.BlockSpec((1, fch, d), lambda e, bi, f: (e, f, 0)),
                  pl.BlockSpec((1, 1, cap, 1), lambda e, bi, f: (bi, e, 0, 0))],
        out_specs=pl.BlockSpec((1, 1, cap, d), lambda e, bi, f: (bi, e, 0, 0)),
        out_shape=jax.ShapeDtypeStruct((b, n_exp, cap, d), F32),
        scratch_shapes=[pltpu.VMEM((cap, d), F32)],
        compiler_params=_cp(("parallel", "parallel", "arbitrary")),
    )(xe, w_gate, w_up, w_down, val)


def _combine_kernel(idx_ref, ye_ref, o_ref, *, cap, n_exp, tq):
    q = pl.program_id(1)
    e = pl.program_id(2)

    @pl.when(e == 0)
    def _():
        o_ref[...] = jnp.zeros_like(o_ref)

    base = (pl.program_id(0) * n_exp + e) * cap
    t0 = q * tq

    def lower_bound(t):
        def step(_, lohi):
            lo, hi = lohi
            live = lo < hi
            mid = (lo + hi) >> 1
            less = idx_ref[base + jnp.minimum(mid, cap - 1)] < t
            return jnp.where(live & less, mid + 1, lo), jnp.where(live & ~less, mid, hi)

        return lax.fori_loop(0, cap.bit_length(), step, (jnp.int32(0), jnp.int32(cap)))[0]

    lo = lower_bound(t0)
    hi = lower_bound(t0 + tq)
    n_groups = (hi - lo) // SCATTER_GROUP

    def group(g, carry):
        p0 = lo + g * SCATTER_GROUP
        rows = [idx_ref[base + p0 + u] - t0 for u in range(SCATTER_GROUP)]
        new = [o_ref[0, pl.ds(i, 1), :] + ye_ref[0, 0, pl.ds(p0 + u, 1), :] for u, i in enumerate(rows)]
        for i, row in zip(rows, new):
            o_ref[0, pl.ds(i, 1), :] = row
        return carry

    lax.fori_loop(0, n_groups, group, 0)

    def single(p, carry):
        i = idx_ref[base + p] - t0
        o_ref[0, pl.ds(i, 1), :] += ye_ref[0, 0, pl.ds(p, 1), :]
        return carry

    lax.fori_loop(lo + n_groups * SCATTER_GROUP, hi, single, 0)


def _combine(idx_flat, ye, s, tq):
    b, n_exp, cap, d = ye.shape
    assert s % tq == 0
    return pl.pallas_call(
        functools.partial(_combine_kernel, cap=cap, n_exp=n_exp, tq=tq),
        grid_spec=pltpu.PrefetchScalarGridSpec(
            num_scalar_prefetch=1, grid=(b, s // tq, n_exp),
            in_specs=[pl.BlockSpec((1, 1, cap, d), lambda bi, q, e, idx: (bi, e, 0, 0))],
            out_specs=pl.BlockSpec((1, tq, d), lambda bi, q, e, idx: (bi, q, 0))),
        out_shape=jax.ShapeDtypeStruct((b, s, d), F32),
        compiler_params=_cp(("parallel", "parallel", "arbitrary")),
    )(idx_flat, ye)


def _final_kernel(x1_ref, moe_ref, g_ref, o_ref):
    x = x1_ref[...] + moe_ref[...]
    ms = jnp.mean(x * x, axis=-1, keepdims=True)
    o_ref[...] = x * lax.rsqrt(ms + RMS_EPS) * g_ref[...]


def _final(x1, moe, g, tm):
    n, d = x1.shape
    row = pl.BlockSpec((tm, d), lambda i: (i, 0))
    return pl.pallas_call(
        _final_kernel,
        grid=(n // tm,),
        in_specs=[row, row, pl.BlockSpec((1, d), lambda i: (0, 0))],
        out_specs=row,
        out_shape=jax.ShapeDtypeStruct((n, d), F32),
        compiler_params=_cp(("parallel",)),
    )(x1, moe, g)


def _tiles(s, ff):
    return dict(
        tm=min(256, s),
        na_rows=min(8, s // GRID_W),
        ff_chunk=min(512, ff),
        tq=max(s // 2, LANES),
    )


def _layer(x, mem, p, g_final):
    b, s, d = x.shape
    na_w = p["w_branch_na"].shape[0]
    rw_w = p["w_branch_rw"].shape[0]
    mem_w = p["w_branch_mem"].shape[0]
    n_exp = p["w_router"].shape[1]
    cap = EC_CAPACITY * s // n_exp
    rw_cols = p["w_in"].shape[1] - 3 * na_w - mem_w - 3 * d
    t = _tiles(s, p["w_exp_gate"].shape[-1])
    bf = lambda name: p[name].astype(BF16)

    q, k, v, rw, memq, gates = _inproj(x.reshape(b * s, d), p["norm_mix_g"].reshape(1, d), bf("w_in"),
                                       na_w, rw_cols, mem_w, d, t["tm"])
    shp = lambda a: a.reshape(b, s, a.shape[-1])
    q, k, v, rw, memq, gates = map(shp, (q, k, v, rw, memq, gates))
    y_na = _na_attention(q, k, v, _na_bias_table(p["na_rpb"]), tr=t["na_rows"])
    y0, y1, b0, b1 = _rwkv(rw, p["rw_mu_rkv"], p["rw_mu_w"], p["rw_mu_a"], p["rw_w0"], p["rw_w2"], p["rw_a0"],
                           p["rw_a2"], p["rw_k_k"], p["rw_k_a"], p["rw_r_k"], rw_w)
    mk, mv = _memkv(mem, p["norm_mem_g"].reshape(1, d), bf("w_mem_kv"), mem_w)
    x1, h2p, aff_t = _merge(x, y_na, y0, y1, b0, b1, rw, memq, mk, mv, gates, p["rw_ln_g"], p["rw_ln_b"],
                            p["rw_g2"], bf("w_branch_na"), bf("w_branch_rw"), bf("w_branch_mem"), bf("w_out"),
                            p["norm_ffn_g"].reshape(1, d), p["w_router"], t["tm"], rw_w)
    idx, val = _topk(aff_t, cap)
    idx_flat = idx.reshape(b * n_exp * cap)
    xe = _gather(idx_flat, h2p, n_exp, cap)
    ye = _ffn(xe, p["w_exp_gate"], p["w_exp_up"], p["w_exp_down"], val.reshape(b, n_exp, cap, 1), t["ff_chunk"])
    moe = _combine(idx_flat, ye, s, t["tq"])
    return _final(x1.reshape(b * s, d), moe.reshape(b * s, d), g_final.reshape(1, d), t["tm"]).reshape(b, s, d)


def kernel(x, mem, norm_mix_g, norm_mem_g, w_in, na_rpb, rw_mu_rkv, rw_mu_w, rw_mu_a, rw_w0, rw_w2, rw_a0, rw_a2,
           rw_k_k, rw_k_a, rw_r_k, rw_g2, rw_ln_g, rw_ln_b, w_mem_kv, w_branch_na, w_branch_rw, w_branch_mem, w_out,
           norm_ffn_g, w_router, w_exp_gate, w_exp_up, w_exp_down, norm_final_g):
    stacked = dict(norm_mix_g=norm_mix_g, norm_mem_g=norm_mem_g, w_in=w_in, na_rpb=na_rpb, rw_mu_rkv=rw_mu_rkv,
                   rw_mu_w=rw_mu_w, rw_mu_a=rw_mu_a, rw_w0=rw_w0, rw_w2=rw_w2, rw_a0=rw_a0, rw_a2=rw_a2,
                   rw_k_k=rw_k_k, rw_k_a=rw_k_a, rw_r_k=rw_r_k, rw_g2=rw_g2, rw_ln_g=rw_ln_g, rw_ln_b=rw_ln_b,
                   w_mem_kv=w_mem_kv, w_branch_na=w_branch_na, w_branch_rw=w_branch_rw, w_branch_mem=w_branch_mem,
                   w_out=w_out, norm_ffn_g=norm_ffn_g, w_router=w_router, w_exp_gate=w_exp_gate,
                   w_exp_up=w_exp_up, w_exp_down=w_exp_down)
    assert w_in.shape[0] == 1, "only a depth-1 stack is supported"
    return _layer(x, mem, {name: a[0] for name, a in stacked.items()}, norm_final_g)
```

```python
import functools

import jax
import jax.numpy as jnp
from jax import lax
from jax.experimental import pallas as pl
from jax.experimental.pallas import tpu as pltpu

F32 = jnp.float32
BF16 = jnp.bfloat16
I32 = jnp.int32
U32 = jnp.uint32
HI = lax.Precision.HIGHEST

HEAD_DIM = 64
GRID_W = 64
NA_WIN_ROWS = 8
NA_WIN_COLS = 16
EC_CAPACITY = 2
RMS_EPS = 1e-6
GN_EPS = 64e-5
NEG_BIG = -1e30

LANES = 128
CHUNK = 64
SCATTER_GROUP = 8
VMEM_LIMIT = 56 * 1024 * 1024


def _cp(sem, vmem=VMEM_LIMIT):
    return pltpu.CompilerParams(dimension_semantics=sem, vmem_limit_bytes=vmem)


def _dot(a, b, prec=None):
    return jnp.dot(a, b, preferred_element_type=F32, precision=prec)


def _dot_nt(a, b, prec=None):
    return lax.dot_general(a, b, (((1,), (1,)), ((), ())), preferred_element_type=F32, precision=prec)


def _dot_tn(a, b, prec=None):
    return lax.dot_general(a, b, (((0,), (0,)), ((), ())), preferred_element_type=F32, precision=prec)


def _sigmoid(x):
    return 1.0 / (1.0 + jnp.exp(-x))


def _split2(x):
    hi = x.astype(BF16)
    return hi, (x - hi.astype(F32)).astype(BF16)


def _split3(x):
    hi, rest = x.astype(BF16), x - x.astype(BF16).astype(F32)
    mid = rest.astype(BF16)
    return hi, mid, (rest - mid.astype(F32)).astype(BF16)


def _dot_split_lhs(x, w_bf):
    n = x.shape[0]
    r = _dot(jnp.concatenate(_split2(x), axis=0), w_bf)
    return r[:n] + r[n:]


def _iota(shape, dim):
    return lax.broadcasted_iota(I32, shape, dim)


def _inproj_kernel(x_ref, g_ref, w_ref, q_ref, k_ref, v_ref, rw_ref, memq_ref, gates_ref, *, cuts):
    x = x_ref[...]
    ms = jnp.mean(x * x, axis=-1, keepdims=True)
    h = (x * lax.rsqrt(ms + RMS_EPS) * g_ref[...]).astype(BF16)
    c = cuts
    q_ref[...] = _dot(h, w_ref[:, c[0]:c[1]]).astype(BF16)
    k_ref[...] = _dot(h, w_ref[:, c[1]:c[2]]).astype(BF16)
    v_ref[...] = _dot(h, w_ref[:, c[2]:c[3]]).astype(BF16)
    rw_ref[...] = _dot(h, w_ref[:, c[3]:c[4]])
    memq_ref[...] = _dot(h, w_ref[:, c[4]:c[5]]).astype(BF16)
    gates_ref[...] = _sigmoid(_dot(h, w_ref[:, c[5]:c[6]])).astype(BF16)


def _inproj(x2, g, w_bf, na_w, rw_w, mem_w, d_model, tm):
    n = x2.shape[0]
    d_in = w_bf.shape[1]
    cuts = (0, na_w, 2 * na_w, 3 * na_w, 3 * na_w + rw_w, 3 * na_w + rw_w + mem_w, d_in)
    assert cuts[6] - cuts[5] == 3 * d_model
    row = lambda w: pl.BlockSpec((tm, w), lambda i: (i, 0))
    return pl.pallas_call(
        functools.partial(_inproj_kernel, cuts=cuts),
        grid=(n // tm,),
        in_specs=[row(d_model), pl.BlockSpec((1, d_model), lambda i: (0, 0)),
                  pl.BlockSpec((d_model, d_in), lambda i: (0, 0))],
        out_specs=[row(na_w), row(na_w), row(na_w), row(rw_w), row(mem_w), row(3 * d_model)],
        out_shape=[jax.ShapeDtypeStruct((n, na_w), BF16)] * 3
        + [jax.ShapeDtypeStruct((n, rw_w), F32), jax.ShapeDtypeStruct((n, mem_w), BF16),
           jax.ShapeDtypeStruct((n, 3 * d_model), BF16)],
        compiler_params=_cp(("parallel",)),
    )(x2, g, w_bf)


def _na_bias_kernel(rpb_ref, o_ref):
    w, kc = GRID_W, NA_WIN_COLS
    shape = (rpb_ref.shape[1], w * w)
    o = _iota(shape, 0)
    flat = _iota(shape, 1)
    qc = flat // w
    c = flat % w
    cs = jnp.clip(qc - kc // 2, 0, w - kc)
    valid = (c >= cs) & (c < cs + kc)
    onehot = ((c - qc + (kc - 1) == o) & valid).astype(F32)
    o_ref[...] = _dot(rpb_ref[...], onehot, HI) + jnp.where(valid[0:1], 0.0, NEG_BIG)


def _na_bias_table(rpb):
    kr, w = NA_WIN_ROWS, GRID_W
    h, n_ro, n_co = rpb.shape
    rows = -(-h * n_ro // 8) * 8
    rpb2 = jnp.zeros((rows, LANES), F32).at[:h * n_ro, :n_co].set(rpb.reshape(h * n_ro, n_co).astype(F32))
    tab = pl.pallas_call(
        _na_bias_kernel,
        out_shape=jax.ShapeDtypeStruct((rows, w * w), F32),
    )(rpb2)[:h * n_ro].reshape(h, n_ro, w, w)
    t = jnp.stack([tab[:, kr - 1 - dl:2 * kr - 1 - dl] for dl in range(kr)], axis=1)
    return t.transpose(0, 1, 3, 2, 4).reshape(h, kr, w, kr * w)


def _na_kernel(q_ref, k_ref, v_ref, bias_ref, o_ref, *, tr, rows, n_pairs):
    j = pl.program_id(1)
    w = GRID_W
    slab = NA_WIN_ROWS * w
    lane = _iota((w, LANES), 1)
    lo_mask = lane < HEAD_DIM

    def row(i, carry):
        r = j * tr + i
        rs = jnp.clip(r - NA_WIN_ROWS // 2, 0, rows - NA_WIN_ROWS)
        delta = r - rs
        q = q_ref[0, pl.ds(pl.multiple_of(i * w, w), w), :] * (HEAD_DIM ** -0.5)
        k0 = pl.multiple_of(rs * w, w)
        kk = k_ref[0, pl.ds(k0, slab), :]
        vv = v_ref[0, pl.ds(k0, slab), :]
        pair = lambda t, p: t[:, LANES * p:LANES * (p + 1)]
        qm = [_stack_pair(pair(q, p), lo_mask) for p in range(n_pairs)]
        s = [_dot_nt(qm[p], pair(kk, p))
             + jnp.concatenate([bias_ref[2 * p, delta], bias_ref[2 * p + 1, delta]], axis=0) for p in range(n_pairs)]
        e = [jnp.exp(si - jnp.max(si, axis=-1, keepdims=True)) for si in s]
        o = [_dot(e[p].astype(BF16), pair(vv, p)) / jnp.sum(e[p], axis=-1, keepdims=True) for p in range(n_pairs)]
        outs = [jnp.where(lo_mask, oi[:w], oi[w:]) for oi in o]
        o_ref[0, pl.ds(pl.multiple_of(i * w, w), w), :] = jnp.concatenate(outs, axis=-1).astype(o_ref.dtype)
        return carry

    lax.fori_loop(0, tr, row, 0, unroll=2)


def _na_attention(q, k, v, bias, tr):
    b, s, c = q.shape
    rows = s // GRID_W
    n_heads = c // HEAD_DIM
    assert n_heads % 2 == 0 and rows >= NA_WIN_ROWS and rows % tr == 0
    return pl.pallas_call(
        functools.partial(_na_kernel, tr=tr, rows=rows, n_pairs=n_heads // 2),
        grid=(b, rows // tr),
        in_specs=[pl.BlockSpec((1, tr * GRID_W, c), lambda bi, j: (bi, j, 0)),
                  pl.BlockSpec((1, s, c), lambda bi, j: (bi, 0, 0)),
                  pl.BlockSpec((1, s, c), lambda bi, j: (bi, 0, 0)),
                  pl.BlockSpec(bias.shape, lambda bi, j: (0, 0, 0, 0))],
        out_specs=pl.BlockSpec((1, tr * GRID_W, c), lambda bi, j: (bi, j, 0)),
        out_shape=jax.ShapeDtypeStruct((b, s, c), BF16),
        compiler_params=_cp(("parallel", "arbitrary")),
    )(q, k, v, bias)


def _softplus(z):
    return jnp.maximum(z, 0.0) + jnp.log(1.0 + jnp.exp(-jnp.abs(z)))


def _stack_pair(xp, lo_mask):
    zero = jnp.zeros_like(xp)
    return jnp.concatenate([jnp.where(lo_mask, xp, zero), jnp.where(lo_mask, zero, xp)], axis=0)


def _rwkv_prep(d, x, edge, prm, rw_w):
    (mu_rkv, mu_wa, w0, wl_hi, a0, wl_lo, k_k, k_a, r_k, bd) = prm
    L = CHUNK
    rowi = _iota((L, 1), 0)
    if d == 0:
        xs = jnp.where(rowi == 0, edge, pltpu.roll(x, 1, 0))
    else:
        xs = jnp.where(rowi == L - 1, edge, pltpu.roll(x, L - 1, 0))

    def mix(lo, hi, mu):
        t = x[:, lo:hi]
        return t + (xs[:, lo:hi] - t) * mu

    r = mix(0, rw_w, mu_rkv[d, 0:1])
    k = mix(rw_w, 2 * rw_w, mu_rkv[d, 1:2])
    v = mix(2 * rw_w, 3 * rw_w, mu_rkv[d, 2:3])
    lwa = mix(3 * rw_w, 3 * rw_w + LANES, mu_wa[d:d + 1])
    lane = _iota((L, LANES), 1)
    tw = jnp.where(lane < HEAD_DIM, jnp.tanh(lwa), lwa)
    tw_hi, tw_lo = _split2(tw)
    hh = _dot(jnp.concatenate([tw_hi, tw_lo], axis=0), wl_hi[d])
    lora = hh[:L] + hh[L:] + _dot(tw_hi, wl_lo[d])
    yield
    w_log = -_softplus(-(w0[d:d + 1] + lora[:, :rw_w])) - 0.5
    logw = -jnp.exp(w_log)
    a = _sigmoid(a0[d:d + 1] + lora[:, rw_w:])
    kk = k * k_k
    k = k * (1.0 + (a - 1.0) * k_a)
    sums = _dot(jnp.concatenate(_split2(kk * kk) + _split2(r * k * r_k), axis=0), bd)
    ti = _iota((L, L), 0)
    si = _iota((L, L), 1)
    tri = ((si <= ti) if d == 0 else (si >= ti)).astype(BF16)
    lc3 = _dot(tri, jnp.concatenate(_split3(logw), axis=1))
    yield
    kk = kk / jnp.maximum(jnp.sqrt(sums[:L] + sums[L:2 * L]), 1e-12)
    bonus = (sums[2 * L:3 * L] + sums[3 * L:]) * v
    bv = kk * a
    lc = lc3[:, :rw_w] + lc3[:, rw_w:2 * rw_w] + lc3[:, 2 * rw_w:]
    tot = jnp.sum(logw, axis=0, keepdims=True)
    e_neg = jnp.exp(-lc)
    e_end = jnp.exp(tot - lc)
    ops = dict(r=r * jnp.exp(lc), a=-kk * jnp.exp(lc - logw), b=bv * e_neg, k=k * e_neg, v=v, bh=bv * e_end,
               kh=k * e_end)
    return ops, jnp.exp(tot), bonus


RWKV_OPERANDS = ("a", "r", "b", "k", "v", "bh", "kh")


def _advance(gens, segments):
    for _ in range(segments):
        for g in gens:
            next(g)


def _finish(gen):
    try:
        while True:
            next(gen)
    except StopIteration as stop:
        return stop.value


def _rwkv_stack(preps, n_pairs):
    lo_mask = _iota((CHUNK, LANES), 1) < HEAD_DIM
    inst = [(d, p) for d in range(len(preps)) for p in range(n_pairs)]
    ops = [[_stack_pair(preps[d][0][name][:, LANES * p:LANES * (p + 1)].astype(BF16), lo_mask) for d, p in inst]
           for name in RWKV_OPERANDS]
    gam = [preps[d][1][:, LANES * p:LANES * (p + 1)] for d, p in inst]
    return ops, gam


def _rwkv_blocks(ops, gam, state_ref, n_dirs, n_pairs):
    L = CHUNK
    n2 = 2 * L
    ii = _iota((n2, n2), 0)
    jj = _iota((n2, n2), 1)
    strict = ((jj < ii), (jj > ii))
    incl = ((jj <= ii), (jj >= ii))
    eye = (ii == jj).astype(F32)
    zero = jnp.zeros((n2, n2), F32)
    inst = [(d, p) for d in range(n_dirs) for p in range(n_pairs)]
    each = lambda f, *ls: [f(*a) for a in zip(*ls)]
    a_s, r_s, b_s, k_s, v_s, bh_s, kh_s = ops
    g = each(lambda a, r, b, k: _dot_nt(jnp.concatenate([a, r], axis=0), jnp.concatenate([b, k], axis=0)),
             a_s, r_s, b_s, k_s)
    n_ab = [jnp.where(strict[d], gi[:n2, :n2], zero) for (d, _), gi in zip(inst, g)]
    a_ak = [jnp.where(strict[d], gi[:n2, n2:], zero).astype(BF16) for (d, _), gi in zip(inst, g)]
    a_rb = [jnp.where(incl[d], gi[n2:, :n2], zero).astype(BF16) for (d, _), gi in zip(inst, g)]
    a_rk = [jnp.where(incl[d], gi[n2:, n2:], zero).astype(BF16) for (d, _), gi in zip(inst, g)]
    yield
    av = each(lambda ak, rk, v: _dot(jnp.concatenate([ak, rk], axis=0), v), a_ak, a_rk, v_s)
    t_inv = [eye + n for n in n_ab]
    npow = each(lambda n: _dot(n, n), [n.astype(BF16) for n in n_ab])
    yield
    for _ in range(CHUNK.bit_length() - 3):
        nb = [n.astype(BF16) for n in npow]
        both = each(lambda t, n: _dot(jnp.concatenate([t.astype(BF16), n], axis=0), n), t_inv, nb)
        t_inv = each(lambda t, x: t + x[:n2], t_inv, both)
        npow = [x[n2:] for x in both]
        yield
    t_inv = each(lambda t, n: t + _dot(t.astype(BF16), n.astype(BF16)), t_inv, npow)
    pq = each(lambda t, a, x: _dot(t.astype(BF16), jnp.concatenate([a, x[:n2].astype(BF16)], axis=1)).astype(BF16),
              t_inv, a_s, av)
    yield
    arb_pq = each(_dot, a_rb, pq)
    ry = each(lambda r, x: (r.astype(F32) + x[:, :LANES]).astype(BF16), r_s, arb_pq)
    y0 = each(lambda x, a: x[:, LANES:] + a[n2:], arb_pq, av)
    yield
    pq_bh = each(_dot_tn, pq, bh_s)
    m_mat = each(lambda gm, x: (eye * gm + x[:LANES]).astype(BF16), gam, pq_bh)
    c_mat = each(lambda x, v, kh: x[LANES:] + _dot_tn(v, kh), pq_bh, v_s, kh_s)
    yield
    s_old = [state_ref[i].astype(BF16) for i in range(len(inst))]
    y = each(lambda r, s, y_: _dot_nt(r, s) + y_, ry, s_old, y0)
    s_new = each(lambda s, m, c: _dot(s, m) + c, s_old, m_mat, c_mat)
    for i, s in enumerate(s_new):
        state_ref[i] = s
    return [jnp.concatenate([yi[:L] + yi[L:] for (di, _), yi in zip(inst, y) if di == d], axis=-1)
            for d in range(n_dirs)]


def _rwkv_kernel(cur0_ref, prev0_ref, cur1_ref, next1_ref, mu_rkv_ref, mu_wa_ref, w0_ref, wlhi_ref, a0_ref,
                 wllo_ref, kk_ref, ka_ref, rk_ref, bd_ref, y0_ref, y1_ref, b0_ref, b1_ref, ops_ref, gam_ref,
                 state_ref, *, n_pairs, rw_w):
    c = pl.program_id(1)

    @pl.when(c == 0)
    def _():
        ops_ref[...] = jnp.zeros_like(ops_ref)
        gam_ref[...] = jnp.zeros_like(gam_ref)
        state_ref[...] = jnp.zeros_like(state_ref)

    n_inst = 2 * n_pairs
    staged = [[ops_ref[n, i] for i in range(n_inst)] for n in range(len(RWKV_OPERANDS))]
    staged_gam = [gam_ref[i] for i in range(n_inst)]

    prm = (mu_rkv_ref[...], mu_wa_ref[...], w0_ref[...], wlhi_ref[...], a0_ref[...], wllo_ref[...],
           kk_ref[...], ka_ref[...], rk_ref[...], bd_ref[...])
    first = jnp.minimum(c, pl.num_programs(1) - 2) == 0
    edge0 = jnp.where(first, 0.0, prev0_ref[0, 7:8, :])
    edge1 = jnp.where(first, 0.0, next1_ref[0, 0:1, :])
    prep = [_rwkv_prep(0, cur0_ref[0], edge0, prm, rw_w), _rwkv_prep(1, cur1_ref[0], edge1, prm, rw_w)]
    blocks = _rwkv_blocks(staged, staged_gam, state_ref, 2, n_pairs)
    _advance([blocks], 2)
    _advance(prep, 1)
    _advance([blocks], 2)
    _advance(prep, 1)
    _advance([blocks], 3)
    preps = [_finish(g) for g in prep]
    b0_ref[0] = preps[0][2]
    b1_ref[0] = preps[1][2]
    ops, gam = _rwkv_stack(preps, n_pairs)
    for n, per_inst in enumerate(ops):
        for i, t in enumerate(per_inst):
            ops_ref[n, i] = t
    for i, t in enumerate(gam):
        gam_ref[i] = t

    ys = _finish(blocks)
    y0_ref[0] = ys[0]
    y1_ref[0] = ys[1]


def _head_block_diag(width):
    h = jnp.arange(width) // HEAD_DIM
    return (h[:, None] == h[None, :]).astype(F32)


def _rwkv(rw, mu_rkv, mu_w, mu_a, w0, w2, a0, a2, k_k, k_a, r_k, rw_w):
    b, s, cols = rw.shape
    assert s % CHUNK == 0 and rw_w % LANES == 0 and mu_w.shape[-1] == HEAD_DIM and mu_a.shape[-1] == HEAD_DIM
    nc = s // CHUNK
    n_pairs = rw_w // LANES
    sub = CHUNK // 8
    mu_wa = jnp.concatenate([mu_w, mu_a], axis=-1)
    wl = jnp.concatenate([jnp.concatenate([w2, jnp.zeros_like(w2)], axis=1),
                          jnp.concatenate([jnp.zeros_like(a2), a2], axis=1)], axis=2)
    wl_hi, wl_lo = _split2(wl)
    bd = _head_block_diag(rw_w).astype(BF16)
    full = lambda a: pl.BlockSpec(a.shape, lambda bi, c: (0,) * a.ndim)
    params = [mu_rkv, mu_wa, w0, wl_hi, a0, wl_lo, k_k.reshape(1, rw_w), k_a.reshape(1, rw_w),
              r_k.reshape(1, rw_w), bd]
    out = jax.ShapeDtypeStruct((b, s, rw_w), F32)
    cp = lambda c: jnp.minimum(c, nc - 1)
    cm = lambda c: jnp.maximum(c - 1, 0)
    blk = lambda chunk_of, rev: pl.BlockSpec(
        (1, CHUNK, rw_w), lambda bi, c: (bi, nc - 1 - chunk_of(c) if rev else chunk_of(c), 0))
    return pl.pallas_call(
        functools.partial(_rwkv_kernel, n_pairs=n_pairs, rw_w=rw_w),
        grid=(b, nc + 1),
        in_specs=[pl.BlockSpec((1, CHUNK, cols), lambda bi, c: (bi, cp(c), 0)),
                  pl.BlockSpec((1, 8, cols), lambda bi, c: (bi, jnp.maximum(cp(c) * sub - 1, 0), 0)),
                  pl.BlockSpec((1, CHUNK, cols), lambda bi, c: (bi, nc - 1 - cp(c), 0)),
                  pl.BlockSpec((1, 8, cols),
                               lambda bi, c: (bi, jnp.minimum((nc - cp(c)) * sub, nc * sub - 1), 0))]
        + [full(a) for a in params],
        out_specs=[blk(cm, False), blk(cm, True), blk(cp, False), blk(cp, True)],
        out_shape=[out] * 4,
        scratch_shapes=[pltpu.VMEM((len(RWKV_OPERANDS), 2 * n_pairs, 2 * CHUNK, LANES), BF16),
                        pltpu.VMEM((2 * n_pairs, 1, LANES), F32),
                        pltpu.VMEM((2 * n_pairs, LANES, LANES), F32)],
        compiler_params=_cp(("parallel", "arbitrary")),
    )(rw, rw, rw, rw, *params)


def _memkv_kernel(mem_ref, g_ref, w_ref, mk_ref, mv_ref, *, mem_w):
    x = mem_ref[0]
    ms = jnp.mean(x * x, axis=-1, keepdims=True)
    h = (x * lax.rsqrt(ms + RMS_EPS) * g_ref[...]).astype(BF16)
    kv = _dot(h, w_ref[...])
    mk_ref[0] = kv[:, :mem_w].astype(BF16)
    mv_ref[0] = kv[:, mem_w:].astype(BF16)


def _memkv(mem, g, w_bf, mem_w):
    b, n_mem, d = mem.shape
    out = jax.ShapeDtypeStruct((b, n_mem, mem_w), BF16)
    return pl.pallas_call(
        functools.partial(_memkv_kernel, mem_w=mem_w),
        grid=(b,),
        in_specs=[pl.BlockSpec((1, n_mem, d), lambda i: (i, 0, 0)), pl.BlockSpec((1, d), lambda i: (0, 0)),
                  pl.BlockSpec(w_bf.shape, lambda i: (0, 0))],
        out_specs=[pl.BlockSpec((1, n_mem, mem_w), lambda i: (i, 0, 0))] * 2,
        out_shape=[out, out],
        compiler_params=_cp(("parallel",)),
    )(mem, g, w_bf)


def _pack_bf16_pairs(h):
    half = h.shape[-1] // 2
    lo = pltpu.bitcast(h[:, :half].astype(BF16).astype(F32), U32)
    hi = pltpu.bitcast(h[:, half:].astype(BF16).astype(F32), U32)
    return (hi & jnp.uint32(0xFFFF0000)) | (lo >> 16)


def _unpack_bf16_pairs(p):
    lo = pltpu.bitcast(p << 16, F32)
    hi = pltpu.bitcast(p & jnp.uint32(0xFFFF0000), F32)
    return jnp.concatenate([lo, hi], axis=-1).astype(BF16)


def _merge_kernel(x_ref, yna_ref, y0_ref, y1_ref, b0_ref, b1_ref, latg_ref, memq_ref, mk_ref, mv_ref, gates_ref,
                  lng_ref, lnb_ref, g2_ref, bdm_ref, wna_ref, wrw_ref, wmem_ref, wout_ref, gffn_ref, wrhi_ref,
                  wrlo_ref,
                  x1_ref, h2_ref, aff_ref, *, d_model, mem_pairs):
    tm = x_ref.shape[1]
    ysum = y0_ref[0] + y1_ref[0]
    bdm = bdm_ref[...]
    mean = _dot_split_lhs(ysum, bdm)
    cen = ysum - mean
    var = _dot_split_lhs(cen * cen, bdm)
    y = cen * lax.rsqrt(var + GN_EPS) * lng_ref[...] + lnb_ref[...] + (b0_ref[0] + b1_ref[0])
    y_rw = y * _dot(_sigmoid(latg_ref[0]).astype(BF16), g2_ref[...])
    lane = _iota((tm, LANES), 1)
    lo_mask = lane < HEAD_DIM
    mq = memq_ref[0]
    mk = mk_ref[0]
    mv = mv_ref[0]
    outs = []
    for p in range(mem_pairs):
        sl = slice(LANES * p, LANES * (p + 1))
        qp = mq[:, sl] * (HEAD_DIM ** -0.5)
        o_h = []
        for hh in range(2):
            qm = jnp.where(lo_mask if hh == 0 else ~lo_mask, qp, jnp.zeros_like(qp))
            s = _dot_nt(qm, mk[:, sl])
            m = jnp.max(s, axis=-1, keepdims=True)
            e = jnp.exp(s - m)
            l = jnp.sum(e, axis=-1, keepdims=True)
            o_h.append(_dot(e.astype(BF16), mv[:, sl]) / l)
        outs.append(jnp.where(lo_mask, o_h[0], o_h[1]))
    y_mem = jnp.concatenate(outs, axis=-1)
    gates = gates_ref[0]
    merged = (gates[:, :d_model] * _dot(yna_ref[0], wna_ref[...])
              + gates[:, d_model:2 * d_model] * _dot(y_rw.astype(BF16), wrw_ref[...])
              + gates[:, 2 * d_model:] * _dot(y_mem.astype(BF16), wmem_ref[...]))
    x1 = x_ref[0] + _dot(merged.astype(BF16), wout_ref[...])
    x1_ref[0] = x1
    ms = jnp.mean(x1 * x1, axis=-1, keepdims=True)
    h2 = x1 * lax.rsqrt(ms + RMS_EPS) * gffn_ref[...]
    h2_ref[0] = _pack_bf16_pairs(h2)
    n_exp = aff_ref.shape[1]
    h_hi, h_lo = _split2(h2)
    hh = _dot(jnp.concatenate([h_hi, h_lo], axis=0), wrhi_ref[...])
    logits = hh[:tm] + hh[tm:] + _dot(h_hi, wrlo_ref[...])
    logits = jnp.where(lane < n_exp, logits, NEG_BIG)
    e = jnp.exp(logits - jnp.max(logits, axis=-1, keepdims=True))
    aff = e / jnp.sum(e, axis=-1, keepdims=True)
    aff_ref[0] = aff.T[:n_exp]


def _merge(x, yna, y0, y1, b0, b1, rw, memq, mk, mv, gates, ln_g, ln_b, g2, wna, wrw, wmem, wout, gffn, w_router,
           tm, rw_w):
    b, s, d = x.shape
    n_exp = w_router.shape[1]
    mem_w = memq.shape[-1]
    latg_w = g2.shape[0]
    assert latg_w == LANES and (rw.shape[-1] - latg_w) % LANES == 0 and n_exp <= LANES
    latg_blk = (rw.shape[-1] - latg_w) // LANES
    bdm = (_head_block_diag(rw_w) / HEAD_DIM).astype(BF16)
    wr = jnp.zeros((d, LANES), F32).at[:, :n_exp].set(w_router)
    wr_hi, wr_lo = _split2(wr)
    tok = lambda w: pl.BlockSpec((1, tm, w), lambda bi, j: (bi, j, 0))
    full = lambda a: pl.BlockSpec(a.shape, lambda bi, j: (0,) * a.ndim)
    per_b = lambda a: pl.BlockSpec((1,) + a.shape[1:], lambda bi, j: (bi, 0, 0))
    weights = [ln_g.reshape(1, rw_w), ln_b.reshape(1, rw_w), g2.astype(BF16), bdm, wna, wrw, wmem, wout, gffn,
               wr_hi, wr_lo]
    return pl.pallas_call(
        functools.partial(_merge_kernel, d_model=d, mem_pairs=mem_w // LANES),
        grid=(b, s // tm),
        in_specs=[tok(d), tok(rw_w), tok(rw_w), tok(rw_w), tok(rw_w), tok(rw_w),
                  pl.BlockSpec((1, tm, latg_w), lambda bi, j: (bi, j, latg_blk)),
                  tok(mem_w), per_b(mk), per_b(mv), tok(3 * d)] + [full(a) for a in weights],
        out_specs=[tok(d), tok(d // 2), pl.BlockSpec((1, n_exp, tm), lambda bi, j: (bi, 0, j))],
        out_shape=[jax.ShapeDtypeStruct((b, s, d), F32), jax.ShapeDtypeStruct((b, s, d // 2), U32),
                   jax.ShapeDtypeStruct((b, n_exp, s), F32)],
        compiler_params=_cp(("parallel", "parallel")),
    )(x, yna, y0, y1, b0, b1, rw, memq, mk, mv, gates, *weights)


def _flat_cumsum(m, upper, lower_strict):
    cr = _dot(m.astype(BF16), upper)
    rowtot = jnp.broadcast_to(cr[:, LANES - 1:LANES], cr.shape)
    rowstart = _dot(lower_strict, rowtot.astype(BF16))
    return cr, rowtot, rowstart


def _topk_kernel(aff_ref, idx_ref, val_ref, thr_ref, *, cap):
    n_exp, rows = aff_ref.shape[1], aff_ref.shape[2]
    bits_all = pltpu.bitcast(aff_ref[0], I32)

    def bisect(_, lohi):
        lo, hi = lohi
        mid = lo + ((hi - lo + 1) >> 1)
        cnt = jnp.sum(jnp.sum((bits_all >= mid).astype(I32), axis=2, keepdims=True), axis=1, keepdims=True)
        ok = cnt >= cap
        return jnp.where(ok, mid, lo), jnp.where(ok, hi, mid - 1)

    thr_all, _ = lax.fori_loop(0, 31, bisect, (jnp.zeros((n_exp, 1, 1), I32),
                                               jnp.full((n_exp, 1, 1), 0x7F800000, I32)))
    thr_ref[...] = jnp.broadcast_to(thr_all, thr_ref.shape)

    upper = (_iota((LANES, LANES), 0) <= _iota((LANES, LANES), 1)).astype(BF16)
    lower_strict = (_iota((rows, rows), 1) < _iota((rows, rows), 0)).astype(BF16)
    lower_incl = (_iota((rows, rows), 1) <= _iota((rows, rows), 0)).astype(BF16)
    p_row = _iota((1, cap), 1).astype(F32)

    def per_expert(e, carry):
        aff = aff_ref[0, e]
        bits = pltpu.bitcast(aff, I32)
        thr = thr_ref[e][0:1, :]
        gt = bits > thr
        tie = bits == thr
        need = (cap - jnp.sum(gt.astype(I32))).astype(F32)
        tie_f = tie.astype(F32)
        cr, _, rowstart = _flat_cumsum(tie_f, upper, lower_strict)
        tie_rank = rowstart + cr - tie_f
        sel = (gt | (tie & (tie_rank < need))).astype(F32)

        cr, rowtot, rowstart = _flat_cumsum(sel, upper, lower_strict)
        rowend = _dot(lower_incl, rowtot.astype(BF16))[:, 0:1]
        r_of_p = jnp.sum((rowend <= p_row).astype(I32), axis=0, keepdims=True)
        onehot = (_iota((rows, cap), 0) == r_of_p).astype(BF16)
        pieces = (cr.astype(BF16),) + _split2(rowstart) + _split3(aff)
        g = _dot_tn(jnp.concatenate(pieces, axis=1), onehot)
        blk = lambda n: g[LANES * n:LANES * (n + 1), :]
        q = p_row - (blk(1) + blk(2))[0:1, :]
        jloc = jnp.sum((blk(0) <= q).astype(I32), axis=0, keepdims=True)
        idx_ref[0, pl.ds(e, 1), :] = r_of_p * LANES + jloc
        val_ref[0, pl.ds(e, 1), :] = jnp.sum(
            jnp.where(_iota((LANES, cap), 0) == jloc, blk(3) + blk(4) + blk(5), 0.0), axis=0, keepdims=True)
        return carry

    lax.fori_loop(0, n_exp, per_expert, 0)


def _topk(aff_t, cap):
    b, n_exp, s = aff_t.shape
    assert s % LANES == 0
    rows = s // LANES
    a4 = aff_t.reshape(b, n_exp, rows, LANES)
    spec = pl.BlockSpec((1, n_exp, cap), lambda bi: (bi, 0, 0))
    return pl.pallas_call(
        functools.partial(_topk_kernel, cap=cap),
        grid=(b,),
        in_specs=[pl.BlockSpec((1, n_exp, rows, LANES), lambda bi: (bi, 0, 0, 0))],
        out_specs=[spec, spec],
        out_shape=[jax.ShapeDtypeStruct((b, n_exp, cap), I32), jax.ShapeDtypeStruct((b, n_exp, cap), F32)],
        scratch_shapes=[pltpu.VMEM((n_exp, 8, LANES), I32)],
        compiler_params=_cp(("parallel",)),
    )(a4)


def _gather_kernel(idx_ref, h_ref, o_ref, *, cap, n_exp):
    base = (pl.program_id(0) * n_exp + pl.program_id(1)) * cap

    def body(p, carry):
        i = idx_ref[base + p]
        o_ref[0, 0, pl.ds(p, 1), :] = h_ref[0, pl.ds(i, 1), :]
        return carry

    lax.fori_loop(0, cap, body, 0, unroll=8)


def _gather(idx_flat, h2p, n_exp, cap):
    b, s, w = h2p.shape
    return pl.pallas_call(
        functools.partial(_gather_kernel, cap=cap, n_exp=n_exp),
        grid_spec=pltpu.PrefetchScalarGridSpec(
            num_scalar_prefetch=1, grid=(b, n_exp),
            in_specs=[pl.BlockSpec((1, s, w), lambda bi, e, idx: (bi, 0, 0))],
            out_specs=pl.BlockSpec((1, 1, cap, w), lambda bi, e, idx: (bi, e, 0, 0))),
        out_shape=jax.ShapeDtypeStruct((b, n_exp, cap, w), U32),
        compiler_params=_cp(("parallel", "arbitrary")),
    )(idx_flat, h2p)


def _ffn_kernel(xe_ref, wg_ref, wu_ref, wd_ref, val_ref, o_ref, acc_ref):
    f = pl.program_id(2)

    @pl.when(f == 0)
    def _():
        acc_ref[...] = jnp.zeros_like(acc_ref)

    xb = _unpack_bf16_pairs(xe_ref[0, 0])
    g = _dot(xb, wg_ref[0].astype(BF16))
    u = _dot(xb, wu_ref[0].astype(BF16))
    act = (g * _sigmoid(g) * u).astype(BF16)
    acc_ref[...] += _dot(act, wd_ref[0].astype(BF16))

    @pl.when(f == pl.num_programs(2) - 1)
    def _():
        o = acc_ref[...] * val_ref[0, 0]
        cap, d = o.shape
        n_t = d // LANES
        for k in range(n_t):
            o_ref[0, 0, pl.ds(k, cap, stride=n_t), :] = o[:, LANES * k:LANES * (k + 1)]


def _ffn(xe, w_gate, w_up, w_down, val, fch):
    b, n_exp, cap, half = xe.shape
    d = 2 * half
    ff = w_gate.shape[-1]
    assert ff % fch == 0
    return pl.pallas_call(
        _ffn_kernel,
        grid=(n_exp, b, ff // fch),
        in_specs=[pl.BlockSpec((1, 1, cap, half), lambda e, bi, f: (bi, e, 0, 0)),
                  pl.BlockSpec((1, d, fch), lambda e, bi, f: (e, 0, f)),
                  pl.BlockSpec((1, d, fch), lambda e, bi, f: (e, 0, f)),
                  pl.BlockSpec((1, fch, d), lambda e, bi, f: (e, f, 0)),
                  pl.BlockSpec((1, 1, cap, 1), lambda e, bi, f: (bi, e, 0, 0))],
        out_specs=pl.BlockSpec((1, 1, cap * (d // LANES), LANES), lambda e, bi, f: (bi, e, 0, 0)),
        out_shape=jax.ShapeDtypeStruct((b, n_exp, cap * (d // LANES), LANES), F32),
        scratch_shapes=[pltpu.VMEM((cap, d), F32)],
        compiler_params=_cp(("parallel", "parallel", "arbitrary")),
    )(xe, w_gate, w_up, w_down, val)


def _combine_kernel(idx_ref, ye_ref, o_ref, *, cap, n_exp, tq, n_t):
    q = pl.program_id(1)
    e = pl.program_id(2)

    @pl.when(e == 0)
    def _():
        o_ref[...] = jnp.zeros_like(o_ref)

    base = (pl.program_id(0) * n_exp + e) * cap
    t0 = q * tq

    def lower_bound(t):
        def step(_, lohi):
            lo, hi = lohi
            live = lo < hi
            mid = (lo + hi) >> 1
            less = idx_ref[base + jnp.minimum(mid, cap - 1)] < t
            return jnp.where(live & less, mid + 1, lo), jnp.where(live & ~less, mid, hi)

        return lax.fori_loop(0, cap.bit_length(), step, (jnp.int32(0), jnp.int32(cap)))[0]

    lo = lower_bound(t0)
    hi = lower_bound(t0 + tq)
    row = lambda r: pl.ds(pl.multiple_of(r * n_t, n_t), n_t)
    n_groups = (hi - lo) // SCATTER_GROUP

    def group(g, carry):
        p0 = lo + g * SCATTER_GROUP
        toks = [idx_ref[base + p0 + u] - t0 for u in range(SCATTER_GROUP)]
        new = [o_ref[0, row(i), :] + ye_ref[0, 0, row(p0 + u), :] for u, i in enumerate(toks)]
        for i, r in zip(toks, new):
            o_ref[0, row(i), :] = r
        return carry

    lax.fori_loop(0, n_groups, group, 0)

    def single(p, carry):
        i = idx_ref[base + p] - t0
        o_ref[0, row(i), :] += ye_ref[0, 0, row(p), :]
        return carry

    lax.fori_loop(lo + n_groups * SCATTER_GROUP, hi, single, 0)


def _combine(idx_flat, ye, cap, s, tq):
    b, n_exp, rows, _ = ye.shape
    n_t = rows // cap
    assert s % tq == 0 and n_t == 8
    return pl.pallas_call(
        functools.partial(_combine_kernel, cap=cap, n_exp=n_exp, tq=tq, n_t=n_t),
        grid_spec=pltpu.PrefetchScalarGridSpec(
            num_scalar_prefetch=1, grid=(b, s // tq, n_exp),
            in_specs=[pl.BlockSpec((1, 1, rows, LANES), lambda bi, q, e, idx: (bi, e, 0, 0))],
            out_specs=pl.BlockSpec((1, tq * n_t, LANES), lambda bi, q, e, idx: (bi, q, 0))),
        out_shape=jax.ShapeDtypeStruct((b, s * n_t, LANES), F32),
        compiler_params=_cp(("parallel", "parallel", "arbitrary")),
    )(idx_flat, ye)


def _final_kernel(x1_ref, moe_ref, g_ref, o_ref):
    tm, d = x1_ref.shape
    n_t = d // LANES
    moe = jnp.concatenate([moe_ref[pl.ds(k, tm, stride=n_t), :] for k in range(n_t)], axis=-1)
    x = x1_ref[...] + moe
    ms = jnp.mean(x * x, axis=-1, keepdims=True)
    o_ref[...] = x * lax.rsqrt(ms + RMS_EPS) * g_ref[...]


def _final(x1, moe_tiles, g, tm):
    n, d = x1.shape
    n_t = d // LANES
    row = pl.BlockSpec((tm, d), lambda i: (i, 0))
    return pl.pallas_call(
        _final_kernel,
        grid=(n // tm,),
        in_specs=[row, pl.BlockSpec((tm * n_t, LANES), lambda i: (i, 0)), pl.BlockSpec((1, d), lambda i: (0, 0))],
        out_specs=row,
        out_shape=jax.ShapeDtypeStruct((n, d), F32),
        compiler_params=_cp(("parallel",)),
    )(x1, moe_tiles, g)


def _tiles(s, ff):
    return dict(
        tm=min(256, s),
        na_rows=min(8, s // GRID_W),
        ff_chunk=min(512, ff),
        tq=max(s // 2, LANES),
    )


def _layer(x, mem, p, g_final):
    b, s, d = x.shape
    na_w = p["w_branch_na"].shape[0]
    rw_w = p["w_branch_rw"].shape[0]
    mem_w = p["w_branch_mem"].shape[0]
    n_exp = p["w_router"].shape[1]
    cap = EC_CAPACITY * s // n_exp
    rw_cols = p["w_in"].shape[1] - 3 * na_w - mem_w - 3 * d
    t = _tiles(s, p["w_exp_gate"].shape[-1])
    bf = lambda name: p[name].astype(BF16)

    q, k, v, rw, memq, gates = _inproj(x.reshape(b * s, d), p["norm_mix_g"].reshape(1, d), bf("w_in"),
                                       na_w, rw_cols, mem_w, d, t["tm"])
    shp = lambda a: a.reshape(b, s, a.shape[-1])
    q, k, v, rw, memq, gates = map(shp, (q, k, v, rw, memq, gates))
    y_na = _na_attention(q, k, v, _na_bias_table(p["na_rpb"]), tr=t["na_rows"])
    y0, y1, b0, b1 = _rwkv(rw, p["rw_mu_rkv"], p["rw_mu_w"], p["rw_mu_a"], p["rw_w0"], p["rw_w2"], p["rw_a0"],
                           p["rw_a2"], p["rw_k_k"], p["rw_k_a"], p["rw_r_k"], rw_w)
    mk, mv = _memkv(mem, p["norm_mem_g"].reshape(1, d), bf("w_mem_kv"), mem_w)
    x1, h2p, aff_t = _merge(x, y_na, y0, y1, b0, b1, rw, memq, mk, mv, gates, p["rw_ln_g"], p["rw_ln_b"],
                            p["rw_g2"], bf("w_branch_na"), bf("w_branch_rw"), bf("w_branch_mem"), bf("w_out"),
                            p["norm_ffn_g"].reshape(1, d), p["w_router"], t["tm"], rw_w)
    idx, val = _topk(aff_t, cap)
    idx_flat = idx.reshape(b * n_exp * cap)
    xe = _gather(idx_flat, h2p, n_exp, cap)
    ye = _ffn(xe, p["w_exp_gate"], p["w_exp_up"], p["w_exp_down"], val.reshape(b, n_exp, cap, 1), t["ff_chunk"])
    moe = _combine(idx_flat, ye, cap, s, t["tq"])
    return _final(x1.reshape(b * s, d), moe.reshape(b * s * (d // LANES), LANES), g_final.reshape(1, d),
                  t["tm"]).reshape(b, s, d)


def kernel(x, mem, norm_mix_g, norm_mem_g, w_in, na_rpb, rw_mu_rkv, rw_mu_w, rw_mu_a, rw_w0, rw_w2, rw_a0, rw_a2,
           rw_k_k, rw_k_a, rw_r_k, rw_g2, rw_ln_g, rw_ln_b, w_mem_kv, w_branch_na, w_branch_rw, w_branch_mem, w_out,
           norm_ffn_g, w_router, w_exp_gate, w_exp_up, w_exp_down, norm_final_g):
    stacked = dict(norm_mix_g=norm_mix_g, norm_mem_g=norm_mem_g, w_in=w_in, na_rpb=na_rpb, rw_mu_rkv=rw_mu_rkv,
                   rw_mu_w=rw_mu_w, rw_mu_a=rw_mu_a, rw_w0=rw_w0, rw_w2=rw_w2, rw_a0=rw_a0, rw_a2=rw_a2,
                   rw_k_k=rw_k_k, rw_k_a=rw_k_a, rw_r_k=rw_r_k, rw_g2=rw_g2, rw_ln_g=rw_ln_g, rw_ln_b=rw_ln_b,
                   w_mem_kv=w_mem_kv, w_branch_na=w_branch_na, w_branch_rw=w_branch_rw, w_branch_mem=w_branch_mem,
                   w_out=w_out, norm_ffn_g=norm_ffn_g, w_router=w_router, w_exp_gate=w_exp_gate,
                   w_exp_up=w_exp_up, w_exp_down=w_exp_down)
    assert w_in.shape[0] == 1, "only a depth-1 stack is supported"
    return _layer(x, mem, {name: a[0] for name, a in stacked.items()}, norm_final_g)
```

```python
import functools

import jax
import jax.numpy as jnp
from jax import lax
from jax.experimental import pallas as pl
from jax.experimental.pallas import tpu as pltpu

F32 = jnp.float32
BF16 = jnp.bfloat16
I32 = jnp.int32
U32 = jnp.uint32
HI = lax.Precision.HIGHEST

HEAD_DIM = 64
GRID_W = 64
NA_WIN_ROWS = 8
NA_WIN_COLS = 16
EC_CAPACITY = 2
RMS_EPS = 1e-6
GN_EPS = 64e-5
NEG_BIG = -1e30

LANES = 128
CHUNK = 64
RWKV_SUB = 4
SCATTER_GROUP = 8
FFN_PARTS = 4
VMEM_LIMIT = 56 * 1024 * 1024


def _cp(sem, vmem=VMEM_LIMIT):
    return pltpu.CompilerParams(dimension_semantics=sem, vmem_limit_bytes=vmem)


def _dot(a, b, prec=None):
    return jnp.dot(a, b, preferred_element_type=F32, precision=prec)


def _dot_nt(a, b, prec=None):
    return lax.dot_general(a, b, (((1,), (1,)), ((), ())), preferred_element_type=F32, precision=prec)


def _dot_tn(a, b, prec=None):
    return lax.dot_general(a, b, (((0,), (0,)), ((), ())), preferred_element_type=F32, precision=prec)


def _sigmoid(x):
    return 1.0 / (1.0 + jnp.exp(-x))


def _split2(x):
    hi = x.astype(BF16)
    return hi, (x - hi.astype(F32)).astype(BF16)


def _split3(x):
    hi, rest = x.astype(BF16), x - x.astype(BF16).astype(F32)
    mid = rest.astype(BF16)
    return hi, mid, (rest - mid.astype(F32)).astype(BF16)


def _iota(shape, dim):
    return lax.broadcasted_iota(I32, shape, dim)


def _inproj_kernel(x_ref, g_ref, w_ref, q_ref, k_ref, v_ref, rw_ref, memq_ref, gates_ref, *, cuts):
    x = x_ref[...]
    ms = jnp.mean(x * x, axis=-1, keepdims=True)
    h = (x * lax.rsqrt(ms + RMS_EPS) * g_ref[...]).astype(BF16)
    c = cuts
    q_ref[...] = _dot(h, w_ref[:, c[0]:c[1]]).astype(BF16)
    k_ref[...] = _dot(h, w_ref[:, c[1]:c[2]]).astype(BF16)
    v_ref[...] = _dot(h, w_ref[:, c[2]:c[3]]).astype(BF16)
    rw_ref[...] = _dot(h, w_ref[:, c[3]:c[4]])
    memq_ref[...] = _dot(h, w_ref[:, c[4]:c[5]]).astype(BF16)
    gates_ref[...] = _sigmoid(_dot(h, w_ref[:, c[5]:c[6]])).astype(BF16)


def _inproj(x2, g, w_bf, na_w, rw_w, mem_w, d_model, tm):
    n = x2.shape[0]
    d_in = w_bf.shape[1]
    cuts = (0, na_w, 2 * na_w, 3 * na_w, 3 * na_w + rw_w, 3 * na_w + rw_w + mem_w, d_in)
    assert cuts[6] - cuts[5] == 3 * d_model
    row = lambda w: pl.BlockSpec((tm, w), lambda i: (i, 0))
    return pl.pallas_call(
        functools.partial(_inproj_kernel, cuts=cuts),
        grid=(n // tm,),
        in_specs=[row(d_model), pl.BlockSpec((1, d_model), lambda i: (0, 0)),
                  pl.BlockSpec((d_model, d_in), lambda i: (0, 0))],
        out_specs=[row(na_w), row(na_w), row(na_w), row(rw_w), row(mem_w), row(3 * d_model)],
        out_shape=[jax.ShapeDtypeStruct((n, na_w), BF16)] * 3
        + [jax.ShapeDtypeStruct((n, rw_w), F32), jax.ShapeDtypeStruct((n, mem_w), BF16),
           jax.ShapeDtypeStruct((n, 3 * d_model), BF16)],
        compiler_params=_cp(("parallel",)),
    )(x2, g, w_bf)


def _na_bias_kernel(rpb_ref, o_ref):
    w, kc = GRID_W, NA_WIN_COLS
    shape = (rpb_ref.shape[1], w * w)
    o = _iota(shape, 0)
    flat = _iota(shape, 1)
    qc = flat // w
    c = flat % w
    cs = jnp.clip(qc - kc // 2, 0, w - kc)
    valid = (c >= cs) & (c < cs + kc)
    onehot = ((c - qc + (kc - 1) == o) & valid).astype(F32)
    o_ref[...] = _dot(rpb_ref[...], onehot, HI) + jnp.where(valid[0:1], 0.0, NEG_BIG)


def _na_bias_table(rpb):
    kr, w = NA_WIN_ROWS, GRID_W
    h, n_ro, n_co = rpb.shape
    rows = -(-h * n_ro // 8) * 8
    rpb2 = jnp.zeros((rows, LANES), F32).at[:h * n_ro, :n_co].set(rpb.reshape(h * n_ro, n_co).astype(F32))
    tab = pl.pallas_call(
        _na_bias_kernel,
        out_shape=jax.ShapeDtypeStruct((rows, w * w), F32),
    )(rpb2)[:h * n_ro].reshape(h, n_ro, w, w)
    t = jnp.stack([tab[:, kr - 1 - dl:2 * kr - 1 - dl] for dl in range(kr)], axis=1)
    return t.transpose(0, 1, 3, 2, 4).reshape(h, kr, w, kr * w)


def _na_kernel(q_ref, k_ref, v_ref, bias_ref, o_ref, *, tr, rows, n_pairs):
    j = pl.program_id(1)
    w = GRID_W
    slab = NA_WIN_ROWS * w
    lane = _iota((w, LANES), 1)
    lo_mask = lane < HEAD_DIM

    def row(i, carry):
        r = j * tr + i
        rs = jnp.clip(r - NA_WIN_ROWS // 2, 0, rows - NA_WIN_ROWS)
        delta = r - rs
        q = q_ref[0, pl.ds(pl.multiple_of(i * w, w), w), :] * (HEAD_DIM ** -0.5)
        k0 = pl.multiple_of(rs * w, w)
        kk = k_ref[0, pl.ds(k0, slab), :]
        vv = v_ref[0, pl.ds(k0, slab), :]
        pair = lambda t, p: t[:, LANES * p:LANES * (p + 1)]
        qm = [_stack_pair(pair(q, p), lo_mask) for p in range(n_pairs)]
        s = [_dot_nt(qm[p], pair(kk, p))
             + jnp.concatenate([bias_ref[2 * p, delta], bias_ref[2 * p + 1, delta]], axis=0) for p in range(n_pairs)]
        e = [jnp.exp(si - jnp.max(si, axis=-1, keepdims=True)) for si in s]
        o = [_dot(e[p].astype(BF16), pair(vv, p)) / jnp.sum(e[p], axis=-1, keepdims=True) for p in range(n_pairs)]
        outs = [jnp.where(lo_mask, oi[:w], oi[w:]) for oi in o]
        o_ref[0, pl.ds(pl.multiple_of(i * w, w), w), :] = jnp.concatenate(outs, axis=-1).astype(o_ref.dtype)
        return carry

    lax.fori_loop(0, tr, row, 0, unroll=2)


def _na_attention(q, k, v, bias, tr):
    b, s, c = q.shape
    rows = s // GRID_W
    n_heads = c // HEAD_DIM
    assert n_heads % 2 == 0 and rows >= NA_WIN_ROWS and rows % tr == 0
    return pl.pallas_call(
        functools.partial(_na_kernel, tr=tr, rows=rows, n_pairs=n_heads // 2),
        grid=(b, rows // tr),
        in_specs=[pl.BlockSpec((1, tr * GRID_W, c), lambda bi, j: (bi, j, 0)),
                  pl.BlockSpec((1, s, c), lambda bi, j: (bi, 0, 0)),
                  pl.BlockSpec((1, s, c), lambda bi, j: (bi, 0, 0)),
                  pl.BlockSpec(bias.shape, lambda bi, j: (0, 0, 0, 0))],
        out_specs=pl.BlockSpec((1, tr * GRID_W, c), lambda bi, j: (bi, j, 0)),
        out_shape=jax.ShapeDtypeStruct((b, s, c), BF16),
        compiler_params=_cp(("parallel", "arbitrary")),
    )(q, k, v, bias)


def _softplus(z):
    return jnp.maximum(z, 0.0) + jnp.log(1.0 + jnp.exp(-jnp.abs(z)))


def _stack_pair(xp, lo_mask):
    zero = jnp.zeros_like(xp)
    return jnp.concatenate([jnp.where(lo_mask, xp, zero), jnp.where(lo_mask, zero, xp)], axis=0)


def _rwkv_prep(d, x, edge, prm, rw_w):
    (mu_rkv, mu_wa, w0, wl_hi, a0, wl_lo, k_k, k_a, r_k, bd) = prm
    L = CHUNK
    rowi = _iota((L, 1), 0)
    if d == 0:
        xs = jnp.where(rowi == 0, edge, pltpu.roll(x, 1, 0))
    else:
        xs = jnp.where(rowi == L - 1, edge, pltpu.roll(x, L - 1, 0))

    def mix(lo, hi, mu):
        t = x[:, lo:hi]
        return t + (xs[:, lo:hi] - t) * mu

    r = mix(0, rw_w, mu_rkv[d, 0:1])
    k = mix(rw_w, 2 * rw_w, mu_rkv[d, 1:2])
    v = mix(2 * rw_w, 3 * rw_w, mu_rkv[d, 2:3])
    lwa = mix(3 * rw_w, 3 * rw_w + LANES, mu_wa[d:d + 1])
    lane = _iota((L, LANES), 1)
    tw = jnp.where(lane < HEAD_DIM, jnp.tanh(lwa), lwa)
    tw_hi, tw_lo = _split2(tw)
    hh = _dot(jnp.concatenate([tw_hi, tw_lo], axis=0), wl_hi[d])
    lora = hh[:L] + hh[L:] + _dot(tw_hi, wl_lo[d])
    yield
    w_log = -_softplus(-(w0[d:d + 1] + lora[:, :rw_w])) - 0.5
    logw = -jnp.exp(w_log)
    a = _sigmoid(a0[d:d + 1] + lora[:, rw_w:])
    kk = k * k_k
    k = k * (1.0 + (a - 1.0) * k_a)
    sums = _dot(jnp.concatenate([(kk * kk).astype(BF16), (r * k * r_k).astype(BF16)], axis=0), bd)
    ti = _iota((L, L), 0)
    si = _iota((L, L), 1)
    tri = ((si <= ti) if d == 0 else (si >= ti)).astype(BF16)
    lc3 = _dot(tri, jnp.concatenate(_split3(logw), axis=1))
    yield
    kk = kk / jnp.maximum(jnp.sqrt(sums[:L]), 1e-12)
    bonus = sums[L:] * v
    bv = kk * a
    lc = lc3[:, :rw_w] + lc3[:, rw_w:2 * rw_w] + lc3[:, 2 * rw_w:]
    tot = jnp.sum(logw, axis=0, keepdims=True)
    e_neg = jnp.exp(-lc)
    e_end = jnp.exp(tot - lc)
    ops = dict(r=r * jnp.exp(lc), a=-kk * jnp.exp(lc - logw), b=bv * e_neg, k=k * e_neg, v=v, bh=bv * e_end,
               kh=k * e_end)
    return ops, jnp.exp(tot), bonus


RWKV_OPERANDS = ("a", "r", "b", "k", "v", "bh", "kh")


def _advance(gens, segments):
    for _ in range(segments):
        for g in gens:
            next(g)


def _finish(gen):
    try:
        while True:
            next(gen)
    except StopIteration as stop:
        return stop.value


def _rwkv_stack(preps, n_pairs):
    lo_mask = _iota((CHUNK, LANES), 1) < HEAD_DIM
    inst = [(d, p) for d in range(len(preps)) for p in range(n_pairs)]
    ops = [[_stack_pair(preps[d][0][name][:, LANES * p:LANES * (p + 1)].astype(BF16), lo_mask) for d, p in inst]
           for name in RWKV_OPERANDS]
    gam = [preps[d][1][:, LANES * p:LANES * (p + 1)] for d, p in inst]
    return ops, gam


def _rwkv_blocks(ops, gam, state_ref, n_dirs, n_sub, n_pairs):
    L = CHUNK
    n2 = 2 * L
    ii = _iota((n2, n2), 0)
    jj = _iota((n2, n2), 1)
    strict = ((jj < ii), (jj > ii))
    incl = ((jj <= ii), (jj >= ii))
    eye = (ii == jj).astype(F32)
    zero = jnp.zeros((n2, n2), F32)
    inst = [(q // n_sub, p) for q in range(n_dirs * n_sub) for p in range(n_pairs)]
    each = lambda f, *ls: [f(*a) for a in zip(*ls)]
    a_s, r_s, b_s, k_s, v_s, bh_s, kh_s = ops
    g = each(lambda a, r, b, k: _dot_nt(jnp.concatenate([a, r], axis=0), jnp.concatenate([b, k], axis=0)),
             a_s, r_s, b_s, k_s)
    n_ab = [jnp.where(strict[d], gi[:n2, :n2], zero) for (d, _), gi in zip(inst, g)]
    a_ak = [jnp.where(strict[d], gi[:n2, n2:], zero).astype(BF16) for (d, _), gi in zip(inst, g)]
    a_rb = [jnp.where(incl[d], gi[n2:, :n2], zero).astype(BF16) for (d, _), gi in zip(inst, g)]
    a_rk = [jnp.where(incl[d], gi[n2:, n2:], zero).astype(BF16) for (d, _), gi in zip(inst, g)]
    yield
    av = each(lambda ak, rk, v: _dot(jnp.concatenate([ak, rk], axis=0), v), a_ak, a_rk, v_s)
    t_inv = [eye + n for n in n_ab]
    npow = each(lambda n: _dot(n, n), [n.astype(BF16) for n in n_ab])
    yield
    for _ in range(CHUNK.bit_length() - 3):
        nb = [n.astype(BF16) for n in npow]
        both = each(lambda t, n: _dot(jnp.concatenate([t.astype(BF16), n], axis=0), n), t_inv, nb)
        t_inv = each(lambda t, x: t + x[:n2], t_inv, both)
        npow = [x[n2:] for x in both]
        yield
    t_inv = each(lambda t, n: t + _dot(t.astype(BF16), n.astype(BF16)), t_inv, npow)
    pq = each(lambda t, a, x: _dot(t.astype(BF16), jnp.concatenate([a, x[:n2].astype(BF16)], axis=1)).astype(BF16),
              t_inv, a_s, av)
    yield
    arb_pq = each(_dot, a_rb, pq)
    ry = each(lambda r, x: (r.astype(F32) + x[:, :LANES]).astype(BF16), r_s, arb_pq)
    y0 = each(lambda x, a: x[:, LANES:] + a[n2:], arb_pq, av)
    yield
    pq_bh = each(_dot_tn, pq, bh_s)
    m_mat = each(lambda gm, x: (eye * gm + x[:LANES]).astype(BF16), gam, pq_bh)
    c_mat = each(lambda x, v, kh: x[LANES:] + _dot_tn(v, kh), pq_bh, v_s, kh_s)
    yield
    state = [state_ref[i] for i in range(n_dirs * n_pairs)]
    y = [None] * len(inst)
    for j in range(n_sub):
        for d in range(n_dirs):
            for p in range(n_pairs):
                i, si = (d * n_sub + j) * n_pairs + p, d * n_pairs + p
                s_old = state[si].astype(BF16)
                y[i] = _dot_nt(ry[i], s_old) + y0[i]
                state[si] = _dot(s_old, m_mat[i]) + c_mat[i]
    for si, s in enumerate(state):
        state_ref[si] = s
    return [jnp.concatenate([y[q * n_pairs + p][:L] + y[q * n_pairs + p][L:] for p in range(n_pairs)], axis=-1)
            for q in range(n_dirs * n_sub)]


def _rwkv_kernel(cur0_ref, prev0_ref, cur1_ref, next1_ref, mu_rkv_ref, mu_wa_ref, w0_ref, wlhi_ref, a0_ref,
                 wllo_ref, kk_ref, ka_ref, rk_ref, bd_ref, y0_ref, y1_ref, b0_ref, b1_ref, ops_ref, gam_ref,
                 state_ref, *, n_pairs, rw_w):
    c = pl.program_id(1)
    L, n_sub = CHUNK, RWKV_SUB

    @pl.when(c == 0)
    def _():
        ops_ref[...] = jnp.zeros_like(ops_ref)
        gam_ref[...] = jnp.zeros_like(gam_ref)
        state_ref[...] = jnp.zeros_like(state_ref)

    n_inst = 2 * n_sub * n_pairs
    staged = [[ops_ref[n, i] for i in range(n_inst)] for n in range(len(RWKV_OPERANDS))]
    staged_gam = [gam_ref[i] for i in range(n_inst)]

    prm = (mu_rkv_ref[...], mu_wa_ref[...], w0_ref[...], wlhi_ref[...], a0_ref[...], wllo_ref[...],
           kk_ref[...], ka_ref[...], rk_ref[...], bd_ref[...])
    first = jnp.minimum(c, pl.num_programs(1) - 2) == 0
    edge0 = jnp.where(first, 0.0, prev0_ref[0, 7:8, :])
    edge1 = jnp.where(first, 0.0, next1_ref[0, 0:1, :])
    x0, x1 = cur0_ref[0], cur1_ref[0]
    rows1 = [(n_sub - 1 - j) * L for j in range(n_sub)]
    prep = ([_rwkv_prep(0, x0[j * L:(j + 1) * L], edge0 if j == 0 else x0[j * L - 1:j * L], prm, rw_w)
             for j in range(n_sub)]
            + [_rwkv_prep(1, x1[r:r + L], edge1 if j == 0 else x1[r + L:r + L + 1], prm, rw_w)
               for j, r in enumerate(rows1)])
    blocks = _rwkv_blocks(staged, staged_gam, state_ref, 2, n_sub, n_pairs)
    _advance([blocks], 2)
    _advance(prep, 1)
    _advance([blocks], 2)
    _advance(prep, 1)
    _advance([blocks], 3)
    preps = [_finish(g) for g in prep]
    b0_ref[0] = jnp.concatenate([preps[j][2] for j in range(n_sub)], axis=0)
    b1_ref[0] = jnp.concatenate([preps[n_sub + j][2] for j in reversed(range(n_sub))], axis=0)
    ops, gam = _rwkv_stack(preps, n_pairs)
    for n, per_inst in enumerate(ops):
        for i, t in enumerate(per_inst):
            ops_ref[n, i] = t
    for i, t in enumerate(gam):
        gam_ref[i] = t

    ys = _finish(blocks)
    y0_ref[0] = jnp.concatenate([ys[j] for j in range(n_sub)], axis=0)
    y1_ref[0] = jnp.concatenate([ys[n_sub + j] for j in reversed(range(n_sub))], axis=0)


def _head_block_diag(width):
    h = jnp.arange(width) // HEAD_DIM
    return (h[:, None] == h[None, :]).astype(F32)


def _rwkv(rw, mu_rkv, mu_w, mu_a, w0, w2, a0, a2, k_k, k_a, r_k, rw_w):
    b, s, cols = rw.shape
    assert s % CHUNK == 0 and rw_w % LANES == 0 and mu_w.shape[-1] == HEAD_DIM and mu_a.shape[-1] == HEAD_DIM
    rows = RWKV_SUB * CHUNK
    assert s % rows == 0
    nc = s // rows
    n_pairs = rw_w // LANES
    sub = rows // 8
    mu_wa = jnp.concatenate([mu_w, mu_a], axis=-1)
    wl = jnp.concatenate([jnp.concatenate([w2, jnp.zeros_like(w2)], axis=1),
                          jnp.concatenate([jnp.zeros_like(a2), a2], axis=1)], axis=2)
    wl_hi, wl_lo = _split2(wl)
    bd = _head_block_diag(rw_w).astype(BF16)
    full = lambda a: pl.BlockSpec(a.shape, lambda bi, c: (0,) * a.ndim)
    params = [mu_rkv, mu_wa, w0, wl_hi, a0, wl_lo, k_k.reshape(1, rw_w), k_a.reshape(1, rw_w),
              r_k.reshape(1, rw_w), bd]
    out = jax.ShapeDtypeStruct((b, s, rw_w), F32)
    cp = lambda c: jnp.minimum(c, nc - 1)
    cm = lambda c: jnp.maximum(c - 1, 0)
    blk = lambda group_of, rev: pl.BlockSpec(
        (1, rows, rw_w), lambda bi, c: (bi, nc - 1 - group_of(c) if rev else group_of(c), 0))
    n_inst = 2 * RWKV_SUB * n_pairs
    return pl.pallas_call(
        functools.partial(_rwkv_kernel, n_pairs=n_pairs, rw_w=rw_w),
        grid=(b, nc + 1),
        in_specs=[pl.BlockSpec((1, rows, cols), lambda bi, c: (bi, cp(c), 0)),
                  pl.BlockSpec((1, 8, cols), lambda bi, c: (bi, jnp.maximum(cp(c) * sub - 1, 0), 0)),
                  pl.BlockSpec((1, rows, cols), lambda bi, c: (bi, nc - 1 - cp(c), 0)),
                  pl.BlockSpec((1, 8, cols),
                               lambda bi, c: (bi, jnp.minimum((nc - cp(c)) * sub, nc * sub - 1), 0))]
        + [full(a) for a in params],
        out_specs=[blk(cm, False), blk(cm, True), blk(cp, False), blk(cp, True)],
        out_shape=[out] * 4,
        scratch_shapes=[pltpu.VMEM((len(RWKV_OPERANDS), n_inst, 2 * CHUNK, LANES), BF16),
                        pltpu.VMEM((n_inst, 1, LANES), F32),
                        pltpu.VMEM((2 * n_pairs, LANES, LANES), F32)],
        compiler_params=_cp(("parallel", "arbitrary")),
    )(rw, rw, rw, rw, *params)


def _memkv_kernel(mem_ref, g_ref, w_ref, mk_ref, mv_ref, *, mem_w):
    x = mem_ref[0]
    ms = jnp.mean(x * x, axis=-1, keepdims=True)
    h = (x * lax.rsqrt(ms + RMS_EPS) * g_ref[...]).astype(BF16)
    kv = _dot(h, w_ref[...])
    mk_ref[0] = kv[:, :mem_w].astype(BF16)
    mv_ref[0] = kv[:, mem_w:].astype(BF16)


def _memkv(mem, g, w_bf, mem_w):
    b, n_mem, d = mem.shape
    out = jax.ShapeDtypeStruct((b, n_mem, mem_w), BF16)
    return pl.pallas_call(
        functools.partial(_memkv_kernel, mem_w=mem_w),
        grid=(b,),
        in_specs=[pl.BlockSpec((1, n_mem, d), lambda i: (i, 0, 0)), pl.BlockSpec((1, d), lambda i: (0, 0)),
                  pl.BlockSpec(w_bf.shape, lambda i: (0, 0))],
        out_specs=[pl.BlockSpec((1, n_mem, mem_w), lambda i: (i, 0, 0))] * 2,
        out_shape=[out, out],
        compiler_params=_cp(("parallel",)),
    )(mem, g, w_bf)


def _merge_kernel(x_ref, yna_ref, y0_ref, y1_ref, b0_ref, b1_ref, latg_ref, memq_ref, mk_ref, mv_ref, gates_ref,
                  lng_ref, lnb_ref, g2_ref, bdm_ref, wna_ref, wrw_ref, wmem_ref, wout_ref, gffn_ref, wrhi_ref,
                  wrlo_ref,
                  x1_ref, h2_ref, aff_ref, *, d_model, mem_pairs):
    tm = x_ref.shape[1]
    ysum = y0_ref[0] + y1_ref[0]
    bdm = bdm_ref[...]
    mean = _dot(ysum.astype(BF16), bdm)
    cen = ysum - mean
    var = _dot((cen * cen).astype(BF16), bdm)
    y = cen * lax.rsqrt(var + GN_EPS) * lng_ref[...] + lnb_ref[...] + (b0_ref[0] + b1_ref[0])
    y_rw = y * _dot(_sigmoid(latg_ref[0]).astype(BF16), g2_ref[...])
    lane = _iota((tm, LANES), 1)
    lo_mask = lane < HEAD_DIM
    mq = memq_ref[0]
    mk = mk_ref[0]
    mv = mv_ref[0]
    outs = []
    for p in range(mem_pairs):
        sl = slice(LANES * p, LANES * (p + 1))
        qp = mq[:, sl] * (HEAD_DIM ** -0.5)
        o_h = []
        for hh in range(2):
            qm = jnp.where(lo_mask if hh == 0 else ~lo_mask, qp, jnp.zeros_like(qp))
            s = _dot_nt(qm, mk[:, sl])
            m = jnp.max(s, axis=-1, keepdims=True)
            e = jnp.exp(s - m)
            l = jnp.sum(e, axis=-1, keepdims=True)
            o_h.append(_dot(e.astype(BF16), mv[:, sl]) / l)
        outs.append(jnp.where(lo_mask, o_h[0], o_h[1]))
    y_mem = jnp.concatenate(outs, axis=-1)
    gates = gates_ref[0]
    merged = (gates[:, :d_model] * _dot(yna_ref[0], wna_ref[...])
              + gates[:, d_model:2 * d_model] * _dot(y_rw.astype(BF16), wrw_ref[...])
              + gates[:, 2 * d_model:] * _dot(y_mem.astype(BF16), wmem_ref[...]))
    x1 = x_ref[0] + _dot(merged.astype(BF16), wout_ref[...])
    x1_ref[0] = x1
    ms = jnp.mean(x1 * x1, axis=-1, keepdims=True)
    h2 = x1 * lax.rsqrt(ms + RMS_EPS) * gffn_ref[...]
    n_t = d_model // LANES
    for k in range(n_t):
        h2_ref[0, pl.ds(k, tm, stride=n_t), :] = h2[:, LANES * k:LANES * (k + 1)]
    n_exp = aff_ref.shape[1]
    h_hi, h_lo = _split2(h2)
    hh = _dot(jnp.concatenate([h_hi, h_lo], axis=0), wrhi_ref[...])
    logits = hh[:tm] + hh[tm:] + _dot(h_hi, wrlo_ref[...])
    logits = jnp.where(lane < n_exp, logits, NEG_BIG)
    e = jnp.exp(logits - jnp.max(logits, axis=-1, keepdims=True))
    aff = e / jnp.sum(e, axis=-1, keepdims=True)
    aff_ref[0] = aff.T[:n_exp]


def _merge(x, yna, y0, y1, b0, b1, rw, memq, mk, mv, gates, ln_g, ln_b, g2, wna, wrw, wmem, wout, gffn, w_router,
           tm, rw_w):
    b, s, d = x.shape
    n_exp = w_router.shape[1]
    mem_w = memq.shape[-1]
    latg_w = g2.shape[0]
    assert latg_w == LANES and (rw.shape[-1] - latg_w) % LANES == 0 and n_exp <= LANES
    latg_blk = (rw.shape[-1] - latg_w) // LANES
    bdm = (_head_block_diag(rw_w) / HEAD_DIM).astype(BF16)
    wr = jnp.zeros((d, LANES), F32).at[:, :n_exp].set(w_router)
    wr_hi, wr_lo = _split2(wr)
    tok = lambda w: pl.BlockSpec((1, tm, w), lambda bi, j: (bi, j, 0))
    full = lambda a: pl.BlockSpec(a.shape, lambda bi, j: (0,) * a.ndim)
    per_b = lambda a: pl.BlockSpec((1,) + a.shape[1:], lambda bi, j: (bi, 0, 0))
    weights = [ln_g.reshape(1, rw_w), ln_b.reshape(1, rw_w), g2.astype(BF16), bdm, wna, wrw, wmem, wout, gffn,
               wr_hi, wr_lo]
    return pl.pallas_call(
        functools.partial(_merge_kernel, d_model=d, mem_pairs=mem_w // LANES),
        grid=(b, s // tm),
        in_specs=[tok(d), tok(rw_w), tok(rw_w), tok(rw_w), tok(rw_w), tok(rw_w),
                  pl.BlockSpec((1, tm, latg_w), lambda bi, j: (bi, j, latg_blk)),
                  tok(mem_w), per_b(mk), per_b(mv), tok(3 * d)] + [full(a) for a in weights],
        out_specs=[tok(d), pl.BlockSpec((1, tm * (d // LANES), LANES), lambda bi, j: (bi, j, 0)),
                   pl.BlockSpec((1, n_exp, tm), lambda bi, j: (bi, 0, j))],
        out_shape=[jax.ShapeDtypeStruct((b, s, d), F32), jax.ShapeDtypeStruct((b, s * (d // LANES), LANES), F32),
                   jax.ShapeDtypeStruct((b, n_exp, s), F32)],
        compiler_params=_cp(("parallel", "parallel")),
    )(x, yna, y0, y1, b0, b1, rw, memq, mk, mv, gates, *weights)


def _flat_cumsum(m, upper, lower_strict):
    cr = _dot(m.astype(BF16), upper)
    rowtot = jnp.broadcast_to(cr[:, LANES - 1:LANES], cr.shape)
    rowstart = _dot(lower_strict, rowtot.astype(BF16))
    return cr, rowtot, rowstart


def _topk_kernel(aff_ref, idx_ref, val_ref, thr_ref, *, cap):
    n_exp, rows = aff_ref.shape[1], aff_ref.shape[2]
    bits_all = pltpu.bitcast(aff_ref[0], I32)

    def bisect(_, lohi):
        lo, hi = lohi
        mid = lo + ((hi - lo + 1) >> 1)
        cnt = jnp.sum(jnp.sum((bits_all >= mid).astype(I32), axis=2, keepdims=True), axis=1, keepdims=True)
        ok = cnt >= cap
        return jnp.where(ok, mid, lo), jnp.where(ok, hi, mid - 1)

    thr_all, _ = lax.fori_loop(0, 31, bisect, (jnp.zeros((n_exp, 1, 1), I32),
                                               jnp.full((n_exp, 1, 1), 0x7F800000, I32)))
    thr_ref[...] = jnp.broadcast_to(thr_all, thr_ref.shape)

    upper = (_iota((LANES, LANES), 0) <= _iota((LANES, LANES), 1)).astype(BF16)
    lower_strict = (_iota((rows, rows), 1) < _iota((rows, rows), 0)).astype(BF16)
    lower_incl = (_iota((rows, rows), 1) <= _iota((rows, rows), 0)).astype(BF16)
    p_row = _iota((1, cap), 1).astype(F32)

    def per_expert(e, carry):
        aff = aff_ref[0, e]
        bits = pltpu.bitcast(aff, I32)
        thr = thr_ref[e][0:1, :]
        gt = bits > thr
        tie = bits == thr
        need = (cap - jnp.sum(gt.astype(I32))).astype(F32)
        tie_f = tie.astype(F32)
        cr, _, rowstart = _flat_cumsum(tie_f, upper, lower_strict)
        tie_rank = rowstart + cr - tie_f
        sel = (gt | (tie & (tie_rank < need))).astype(F32)

        cr, rowtot, rowstart = _flat_cumsum(sel, upper, lower_strict)
        rowend = _dot(lower_incl, rowtot.astype(BF16))[:, 0:1]
        r_of_p = jnp.sum((rowend <= p_row).astype(I32), axis=0, keepdims=True)
        onehot = (_iota((rows, cap), 0) == r_of_p).astype(BF16)
        pieces = (cr.astype(BF16),) + _split2(rowstart) + _split3(aff)
        g = _dot_tn(jnp.concatenate(pieces, axis=1), onehot)
        blk = lambda n: g[LANES * n:LANES * (n + 1), :]
        q = p_row - (blk(1) + blk(2))[0:1, :]
        jloc = jnp.sum((blk(0) <= q).astype(I32), axis=0, keepdims=True)
        idx_ref[0, pl.ds(e, 1), :] = r_of_p * LANES + jloc
        val_ref[0, pl.ds(e, 1), :] = jnp.sum(
            jnp.where(_iota((LANES, cap), 0) == jloc, blk(3) + blk(4) + blk(5), 0.0), axis=0, keepdims=True)
        return carry

    lax.fori_loop(0, n_exp, per_expert, 0)


def _topk(aff_t, cap):
    b, n_exp, s = aff_t.shape
    assert s % LANES == 0
    rows = s // LANES
    a4 = aff_t.reshape(b, n_exp, rows, LANES)
    spec = pl.BlockSpec((1, n_exp, cap), lambda bi: (bi, 0, 0))
    return pl.pallas_call(
        functools.partial(_topk_kernel, cap=cap),
        grid=(b,),
        in_specs=[pl.BlockSpec((1, n_exp, rows, LANES), lambda bi: (bi, 0, 0, 0))],
        out_specs=[spec, spec],
        out_shape=[jax.ShapeDtypeStruct((b, n_exp, cap), I32), jax.ShapeDtypeStruct((b, n_exp, cap), F32)],
        scratch_shapes=[pltpu.VMEM((n_exp, 8, LANES), I32)],
        compiler_params=_cp(("parallel",)),
    )(a4)


def _gather_kernel(idx_ref, h_ref, o_ref, *, cap, n_exp, n_t):
    base = (pl.program_id(0) * n_exp + pl.program_id(1)) * cap
    row = lambda r: pl.ds(pl.multiple_of(r * n_t, n_t), n_t)

    def body(p, carry):
        o_ref[0, 0, row(p), :] = h_ref[0, row(idx_ref[base + p]), :]
        return carry

    lax.fori_loop(0, cap, body, 0, unroll=8)


def _gather(idx_flat, h_tiles, n_exp, cap, n_t):
    b, rows, _ = h_tiles.shape
    assert n_t == 8
    return pl.pallas_call(
        functools.partial(_gather_kernel, cap=cap, n_exp=n_exp, n_t=n_t),
        grid_spec=pltpu.PrefetchScalarGridSpec(
            num_scalar_prefetch=1, grid=(b, n_exp),
            in_specs=[pl.BlockSpec((1, rows, LANES), lambda bi, e, idx: (bi, 0, 0), pipeline_mode=pl.Buffered(1))],
            out_specs=pl.BlockSpec((1, 1, cap * n_t, LANES), lambda bi, e, idx: (bi, e, 0, 0))),
        out_shape=jax.ShapeDtypeStruct((b, n_exp, cap * n_t, LANES), F32),
        compiler_params=_cp(("parallel", "arbitrary")),
    )(idx_flat, h_tiles)


def _ffn_kernel(xe_ref, wg_ref, wu_ref, wd_ref, val_ref, o_ref, acc_ref, xb_ref):
    f = pl.program_id(2)

    @pl.when(f == 0)
    def _():
        acc_ref[...] = jnp.zeros_like(acc_ref)
        n_t = xb_ref.shape[1] // LANES
        rows = xb_ref.shape[0]
        for k in range(n_t):
            xb_ref[:, LANES * k:LANES * (k + 1)] = xe_ref[0, 0, pl.ds(k, rows, stride=n_t), :].astype(BF16)

    xb = xb_ref[...]
    w = wg_ref.shape[2] // FFN_PARTS
    part = lambda j: slice(j * w, (j + 1) * w)
    gu = [(_dot(xb, wg_ref[0, :, part(j)].astype(BF16)), _dot(xb, wu_ref[0, :, part(j)].astype(BF16)))
          for j in range(FFN_PARTS)]
    act = [(g * _sigmoid(g) * u).astype(BF16) for g, u in gu]
    acc_ref[...] += sum(_dot(a, wd_ref[0, part(j), :].astype(BF16)) for j, a in enumerate(act))

    @pl.when(f == pl.num_programs(2) - 1)
    def _():
        o = acc_ref[...] * val_ref[0, 0]
        cap, d = o.shape
        n_t = d // LANES
        for k in range(n_t):
            o_ref[0, 0, pl.ds(k, cap, stride=n_t), :] = o[:, LANES * k:LANES * (k + 1)]


def _ffn(xe, w_gate, w_up, w_down, val, fch):
    b, n_exp, cap = val.shape[:3]
    d = w_gate.shape[1]
    ff = w_gate.shape[-1]
    assert ff % fch == 0
    return pl.pallas_call(
        _ffn_kernel,
        grid=(n_exp, b, ff // fch),
        in_specs=[pl.BlockSpec((1, 1, cap * (d // LANES), LANES), lambda e, bi, f: (bi, e, 0, 0)),
                  pl.BlockSpec((1, d, fch), lambda e, bi, f: (e, 0, f)),
                  pl.BlockSpec((1, d, fch), lambda e, bi, f: (e, 0, f)),
                  pl.BlockSpec((1, fch, d), lambda e, bi, f: (e, f, 0)),
                  pl.BlockSpec((1, 1, cap, 1), lambda e, bi, f: (bi, e, 0, 0))],
        out_specs=pl.BlockSpec((1, 1, cap * (d // LANES), LANES), lambda e, bi, f: (bi, e, 0, 0)),
        out_shape=jax.ShapeDtypeStruct((b, n_exp, cap * (d // LANES), LANES), F32),
        scratch_shapes=[pltpu.VMEM((cap, d), F32), pltpu.VMEM((cap, d), BF16)],
        compiler_params=_cp(("parallel", "parallel", "arbitrary")),
    )(xe, w_gate, w_up, w_down, val)


def _combine_kernel(idx_ref, ye_ref, o_ref, *, cap, n_exp, tq, n_t):
    q = pl.program_id(1)
    e = pl.program_id(2)

    @pl.when(e == 0)
    def _():
        o_ref[...] = jnp.zeros_like(o_ref)

    base = (pl.program_id(0) * n_exp + e) * cap
    t0 = q * tq

    def lower_bound(t):
        def step(_, lohi):
            lo, hi = lohi
            live = lo < hi
            mid = (lo + hi) >> 1
            less = idx_ref[base + jnp.minimum(mid, cap - 1)] < t
            return jnp.where(live & less, mid + 1, lo), jnp.where(live & ~less, mid, hi)

        return lax.fori_loop(0, cap.bit_length(), step, (jnp.int32(0), jnp.int32(cap)))[0]

    lo = lower_bound(t0)
    hi = lower_bound(t0 + tq)
    row = lambda r: pl.ds(pl.multiple_of(r * n_t, n_t), n_t)
    n_groups = (hi - lo) // SCATTER_GROUP

    def group(g, carry):
        p0 = lo + g * SCATTER_GROUP
        toks = [idx_ref[base + p0 + u] - t0 for u in range(SCATTER_GROUP)]
        new = [o_ref[0, row(i), :] + ye_ref[0, 0, row(p0 + u), :] for u, i in enumerate(toks)]
        for i, r in zip(toks, new):
            o_ref[0, row(i), :] = r
        return carry

    lax.fori_loop(0, n_groups, group, 0)

    def single(p, carry):
        i = idx_ref[base + p] - t0
        o_ref[0, row(i), :] += ye_ref[0, 0, row(p), :]
        return carry

    lax.fori_loop(lo + n_groups * SCATTER_GROUP, hi, single, 0)


def _combine(idx_flat, ye, cap, s, tq):
    b, n_exp, rows, _ = ye.shape
    n_t = rows // cap
    assert s % tq == 0 and n_t == 8
    return pl.pallas_call(
        functools.partial(_combine_kernel, cap=cap, n_exp=n_exp, tq=tq, n_t=n_t),
        grid_spec=pltpu.PrefetchScalarGridSpec(
            num_scalar_prefetch=1, grid=(b, s // tq, n_exp),
            in_specs=[pl.BlockSpec((1, 1, rows, LANES), lambda bi, q, e, idx: (bi, e, 0, 0))],
            out_specs=pl.BlockSpec((1, tq * n_t, LANES), lambda bi, q, e, idx: (bi, q, 0))),
        out_shape=jax.ShapeDtypeStruct((b, s * n_t, LANES), F32),
        compiler_params=_cp(("parallel", "parallel", "arbitrary")),
    )(idx_flat, ye)


def _final_kernel(x1_ref, moe_ref, g_ref, o_ref):
    tm, d = x1_ref.shape
    n_t = d // LANES
    moe = jnp.concatenate([moe_ref[pl.ds(k, tm, stride=n_t), :] for k in range(n_t)], axis=-1)
    x = x1_ref[...] + moe
    ms = jnp.mean(x * x, axis=-1, keepdims=True)
    o_ref[...] = x * lax.rsqrt(ms + RMS_EPS) * g_ref[...]


def _final(x1, moe_tiles, g, tm):
    n, d = x1.shape
    n_t = d // LANES
    row = pl.BlockSpec((tm, d), lambda i: (i, 0))
    return pl.pallas_call(
        _final_kernel,
        grid=(n // tm,),
        in_specs=[row, pl.BlockSpec((tm * n_t, LANES), lambda i: (i, 0)), pl.BlockSpec((1, d), lambda i: (0, 0))],
        out_specs=row,
        out_shape=jax.ShapeDtypeStruct((n, d), F32),
        compiler_params=_cp(("parallel",)),
    )(x1, moe_tiles, g)


def _tiles(s, ff):
    return dict(
        tm=min(256, s),
        tm_final=min(512, s),
        na_rows=min(16, s // GRID_W),
        ff_chunk=min(1024, ff),
        tq=max(s // 2, LANES),
    )


def _layer(x, mem, p, g_final):
    b, s, d = x.shape
    na_w = p["w_branch_na"].shape[0]
    rw_w = p["w_branch_rw"].shape[0]
    mem_w = p["w_branch_mem"].shape[0]
    n_exp = p["w_router"].shape[1]
    cap = EC_CAPACITY * s // n_exp
    rw_cols = p["w_in"].shape[1] - 3 * na_w - mem_w - 3 * d
    t = _tiles(s, p["w_exp_gate"].shape[-1])
    bf = lambda name: p[name].astype(BF16)

    q, k, v, rw, memq, gates = _inproj(x.reshape(b * s, d), p["norm_mix_g"].reshape(1, d), bf("w_in"),
                                       na_w, rw_cols, mem_w, d, t["tm"])
    shp = lambda a: a.reshape(b, s, a.shape[-1])
    q, k, v, rw, memq, gates = map(shp, (q, k, v, rw, memq, gates))
    y_na = _na_attention(q, k, v, _na_bias_table(p["na_rpb"]), tr=t["na_rows"])
    y0, y1, b0, b1 = _rwkv(rw, p["rw_mu_rkv"], p["rw_mu_w"], p["rw_mu_a"], p["rw_w0"], p["rw_w2"], p["rw_a0"],
                           p["rw_a2"], p["rw_k_k"], p["rw_k_a"], p["rw_r_k"], rw_w)
    mk, mv = _memkv(mem, p["norm_mem_g"].reshape(1, d), bf("w_mem_kv"), mem_w)
    x1, h2p, aff_t = _merge(x, y_na, y0, y1, b0, b1, rw, memq, mk, mv, gates, p["rw_ln_g"], p["rw_ln_b"],
                            p["rw_g2"], bf("w_branch_na"), bf("w_branch_rw"), bf("w_branch_mem"), bf("w_out"),
                            p["norm_ffn_g"].reshape(1, d), p["w_router"], t["tm"], rw_w)
    idx, val = _topk(aff_t, cap)
    idx_flat = idx.reshape(b * n_exp * cap)
    xe = _gather(idx_flat, h2p, n_exp, cap, d // LANES)
    ye = _ffn(xe, p["w_exp_gate"], p["w_exp_up"], p["w_exp_down"], val.reshape(b, n_exp, cap, 1), t["ff_chunk"])
    moe = _combine(idx_flat, ye, cap, s, t["tq"])
    return _final(x1.reshape(b * s, d), moe.reshape(b * s * (d // LANES), LANES), g_final.reshape(1, d),
                  t["tm_final"]).reshape(b, s, d)


def kernel(x, mem, norm_mix_g, norm_mem_g, w_in, na_rpb, rw_mu_rkv, rw_mu_w, rw_mu_a, rw_w0, rw_w2, rw_a0, rw_a2,
           rw_k_k, rw_k_a, rw_r_k, rw_g2, rw_ln_g, rw_ln_b, w_mem_kv, w_branch_na, w_branch_rw, w_branch_mem, w_out,
           norm_ffn_g, w_router, w_exp_gate, w_exp_up, w_exp_down, norm_final_g):
    stacked = dict(norm_mix_g=norm_mix_g, norm_mem_g=norm_mem_g, w_in=w_in, na_rpb=na_rpb, rw_mu_rkv=rw_mu_rkv,
                   rw_mu_w=rw_mu_w, rw_mu_a=rw_mu_a, rw_w0=rw_w0, rw_w2=rw_w2, rw_a0=rw_a0, rw_a2=rw_a2,
                   rw_k_k=rw_k_k, rw_k_a=rw_k_a, rw_r_k=rw_r_k, rw_g2=rw_g2, rw_ln_g=rw_ln_g, rw_ln_b=rw_ln_b,
                   w_mem_kv=w_mem_kv, w_branch_na=w_branch_na, w_branch_rw=w_branch_rw, w_branch_mem=w_branch_mem,
                   w_out=w_out, norm_ffn_g=norm_ffn_g, w_router=w_router, w_exp_gate=w_exp_gate,
                   w_exp_up=w_exp_up, w_exp_down=w_exp_down)
    assert w_in.shape[0] == 1, "only a depth-1 stack is supported"
    return _layer(x, mem, {name: a[0] for name, a in stacked.items()}, norm_final_g)
```

```python
import functools

import jax
import jax.numpy as jnp
from jax import lax
from jax.experimental import pallas as pl
from jax.experimental.pallas import tpu as pltpu

F32 = jnp.float32
BF16 = jnp.bfloat16
I32 = jnp.int32
U32 = jnp.uint32
HI = lax.Precision.HIGHEST

HEAD_DIM = 64
GRID_W = 64
NA_WIN_ROWS = 8
NA_WIN_COLS = 16
EC_CAPACITY = 2
RMS_EPS = 1e-6
GN_EPS = 64e-5
NEG_BIG = -1e30

LANES = 128
CHUNK = 64
RWKV_SUB = 4
SCATTER_GROUP = 8
FFN_PARTS = 4
VMEM_LIMIT = 56 * 1024 * 1024


def _cp(sem, vmem=VMEM_LIMIT):
    return pltpu.CompilerParams(dimension_semantics=sem, vmem_limit_bytes=vmem)


def _dot(a, b, prec=None):
    return jnp.dot(a, b, preferred_element_type=F32, precision=prec)


def _dot_nt(a, b, prec=None):
    return lax.dot_general(a, b, (((1,), (1,)), ((), ())), preferred_element_type=F32, precision=prec)


def _dot_tn(a, b, prec=None):
    return lax.dot_general(a, b, (((0,), (0,)), ((), ())), preferred_element_type=F32, precision=prec)


def _sigmoid(x):
    return 1.0 / (1.0 + jnp.exp(-x))


def _split2(x):
    hi = x.astype(BF16)
    return hi, (x - hi.astype(F32)).astype(BF16)


def _split3(x):
    hi, rest = x.astype(BF16), x - x.astype(BF16).astype(F32)
    mid = rest.astype(BF16)
    return hi, mid, (rest - mid.astype(F32)).astype(BF16)


def _iota(shape, dim):
    return lax.broadcasted_iota(I32, shape, dim)


def _inproj_kernel(x_ref, g_ref, w_ref, q_ref, k_ref, v_ref, rw_ref, memq_ref, gates_ref, *, cuts):
    x = x_ref[...]
    ms = jnp.mean(x * x, axis=-1, keepdims=True)
    h = (x * lax.rsqrt(ms + RMS_EPS) * g_ref[...]).astype(BF16)
    c = cuts
    q_ref[...] = _dot(h, w_ref[:, c[0]:c[1]]).astype(BF16)
    k_ref[...] = _dot(h, w_ref[:, c[1]:c[2]]).astype(BF16)
    v_ref[...] = _dot(h, w_ref[:, c[2]:c[3]]).astype(BF16)
    rw_ref[...] = _dot(h, w_ref[:, c[3]:c[4]])
    memq_ref[...] = _dot(h, w_ref[:, c[4]:c[5]]).astype(BF16)
    gates_ref[...] = _sigmoid(_dot(h, w_ref[:, c[5]:c[6]])).astype(BF16)


def _inproj(x2, g, w_bf, na_w, rw_w, mem_w, d_model, tm):
    n = x2.shape[0]
    d_in = w_bf.shape[1]
    cuts = (0, na_w, 2 * na_w, 3 * na_w, 3 * na_w + rw_w, 3 * na_w + rw_w + mem_w, d_in)
    assert cuts[6] - cuts[5] == 3 * d_model
    row = lambda w: pl.BlockSpec((tm, w), lambda i: (i, 0))
    return pl.pallas_call(
        functools.partial(_inproj_kernel, cuts=cuts),
        grid=(n // tm,),
        in_specs=[row(d_model), pl.BlockSpec((1, d_model), lambda i: (0, 0)),
                  pl.BlockSpec((d_model, d_in), lambda i: (0, 0))],
        out_specs=[row(na_w), row(na_w), row(na_w), row(rw_w), row(mem_w), row(3 * d_model)],
        out_shape=[jax.ShapeDtypeStruct((n, na_w), BF16)] * 3
        + [jax.ShapeDtypeStruct((n, rw_w), F32), jax.ShapeDtypeStruct((n, mem_w), BF16),
           jax.ShapeDtypeStruct((n, 3 * d_model), BF16)],
        compiler_params=_cp(("parallel",)),
    )(x2, g, w_bf)


def _na_bias_kernel(rpb_ref, o_ref):
    w, kc = GRID_W, NA_WIN_COLS
    shape = (rpb_ref.shape[1], w * w)
    o = _iota(shape, 0)
    flat = _iota(shape, 1)
    qc = flat // w
    c = flat % w
    cs = jnp.clip(qc - kc // 2, 0, w - kc)
    valid = (c >= cs) & (c < cs + kc)
    onehot = ((c - qc + (kc - 1) == o) & valid).astype(F32)
    o_ref[...] = _dot(rpb_ref[...], onehot, HI) + jnp.where(valid[0:1], 0.0, NEG_BIG)


def _na_bias_table(rpb):
    kr, w = NA_WIN_ROWS, GRID_W
    h, n_ro, n_co = rpb.shape
    rows = -(-h * n_ro // 8) * 8
    rpb2 = jnp.zeros((rows, LANES), F32).at[:h * n_ro, :n_co].set(rpb.reshape(h * n_ro, n_co).astype(F32))
    tab = pl.pallas_call(
        _na_bias_kernel,
        out_shape=jax.ShapeDtypeStruct((rows, w * w), F32),
    )(rpb2)[:h * n_ro].reshape(h, n_ro, w, w)
    t = jnp.stack([tab[:, kr - 1 - dl:2 * kr - 1 - dl] for dl in range(kr)], axis=1)
    return t.transpose(0, 1, 3, 2, 4).reshape(h, kr, w, kr * w)


def _na_kernel(q_ref, k_ref, v_ref, bias_ref, o_ref, *, tr, rows, n_pairs):
    j = pl.program_id(1)
    w = GRID_W
    slab = NA_WIN_ROWS * w
    lane = _iota((w, LANES), 1)
    lo_mask = lane < HEAD_DIM

    def row(i, carry):
        r = j * tr + i
        rs = jnp.clip(r - NA_WIN_ROWS // 2, 0, rows - NA_WIN_ROWS)
        delta = r - rs
        q = q_ref[0, pl.ds(pl.multiple_of(i * w, w), w), :] * (HEAD_DIM ** -0.5)
        k0 = pl.multiple_of(rs * w, w)
        kk = k_ref[0, pl.ds(k0, slab), :]
        vv = v_ref[0, pl.ds(k0, slab), :]
        pair = lambda t, p: t[:, LANES * p:LANES * (p + 1)]
        qm = [_stack_pair(pair(q, p), lo_mask) for p in range(n_pairs)]
        s = [_dot_nt(qm[p], pair(kk, p))
             + jnp.concatenate([bias_ref[2 * p, delta], bias_ref[2 * p + 1, delta]], axis=0) for p in range(n_pairs)]
        e = [jnp.exp(si - jnp.max(si, axis=-1, keepdims=True)) for si in s]
        o = [_dot(e[p].astype(BF16), pair(vv, p)) / jnp.sum(e[p], axis=-1, keepdims=True) for p in range(n_pairs)]
        outs = [jnp.where(lo_mask, oi[:w], oi[w:]) for oi in o]
        o_ref[0, pl.ds(pl.multiple_of(i * w, w), w), :] = jnp.concatenate(outs, axis=-1).astype(o_ref.dtype)
        return carry

    lax.fori_loop(0, tr, row, 0, unroll=2)


def _na_attention(q, k, v, bias, tr):
    b, s, c = q.shape
    rows = s // GRID_W
    n_heads = c // HEAD_DIM
    assert n_heads % 2 == 0 and rows >= NA_WIN_ROWS and rows % tr == 0
    return pl.pallas_call(
        functools.partial(_na_kernel, tr=tr, rows=rows, n_pairs=n_heads // 2),
        grid=(b, rows // tr),
        in_specs=[pl.BlockSpec((1, tr * GRID_W, c), lambda bi, j: (bi, j, 0)),
                  pl.BlockSpec((1, s, c), lambda bi, j: (bi, 0, 0)),
                  pl.BlockSpec((1, s, c), lambda bi, j: (bi, 0, 0)),
                  pl.BlockSpec(bias.shape, lambda bi, j: (0, 0, 0, 0))],
        out_specs=pl.BlockSpec((1, tr * GRID_W, c), lambda bi, j: (bi, j, 0)),
        out_shape=jax.ShapeDtypeStruct((b, s, c), BF16),
        compiler_params=_cp(("parallel", "arbitrary")),
    )(q, k, v, bias)


def _softplus(z):
    return jnp.maximum(z, 0.0) + jnp.log(1.0 + jnp.exp(-jnp.abs(z)))


def _stack_pair(xp, lo_mask):
    zero = jnp.zeros_like(xp)
    return jnp.concatenate([jnp.where(lo_mask, xp, zero), jnp.where(lo_mask, zero, xp)], axis=0)


def _rwkv_prep(d, x, edge, prm, rw_w):
    (mu_rkv, mu_wa, w0, wl_hi, a0, wl_lo, k_k, k_a, r_k, bd) = prm
    L = CHUNK
    rowi = _iota((L, 1), 0)
    if d == 0:
        xs = jnp.where(rowi == 0, edge, pltpu.roll(x, 1, 0))
    else:
        xs = jnp.where(rowi == L - 1, edge, pltpu.roll(x, L - 1, 0))

    def mix(lo, hi, mu):
        t = x[:, lo:hi]
        return t + (xs[:, lo:hi] - t) * mu

    r = mix(0, rw_w, mu_rkv[d, 0:1])
    k = mix(rw_w, 2 * rw_w, mu_rkv[d, 1:2])
    v = mix(2 * rw_w, 3 * rw_w, mu_rkv[d, 2:3])
    lwa = mix(3 * rw_w, 3 * rw_w + LANES, mu_wa[d:d + 1])
    lane = _iota((L, LANES), 1)
    tw = jnp.where(lane < HEAD_DIM, jnp.tanh(lwa), lwa)
    tw_hi, tw_lo = _split2(tw)
    hh = _dot(jnp.concatenate([tw_hi, tw_lo], axis=0), wl_hi[d])
    lora = hh[:L] + hh[L:] + _dot(tw_hi, wl_lo[d])
    yield
    w_log = -_softplus(-(w0[d:d + 1] + lora[:, :rw_w])) - 0.5
    logw = -jnp.exp(w_log)
    a = _sigmoid(a0[d:d + 1] + lora[:, rw_w:])
    kk = k * k_k
    k = k * (1.0 + (a - 1.0) * k_a)
    sums = _dot(jnp.concatenate([(kk * kk).astype(BF16), (r * k * r_k).astype(BF16)], axis=0), bd)
    ti = _iota((L, L), 0)
    si = _iota((L, L), 1)
    tri = ((si <= ti) if d == 0 else (si >= ti)).astype(BF16)
    lc3 = _dot(tri, jnp.concatenate(_split3(logw), axis=1))
    yield
    kk = kk / jnp.maximum(jnp.sqrt(sums[:L]), 1e-12)
    bonus = sums[L:] * v
    bv = kk * a
    lc = lc3[:, :rw_w] + lc3[:, rw_w:2 * rw_w] + lc3[:, 2 * rw_w:]
    tot = jnp.sum(logw, axis=0, keepdims=True)
    e_neg = jnp.exp(-lc)
    e_end = jnp.exp(tot - lc)
    ops = dict(r=r * jnp.exp(lc), a=-kk * jnp.exp(lc - logw), b=bv * e_neg, k=k * e_neg, v=v, bh=bv * e_end,
               kh=k * e_end)
    return ops, jnp.exp(tot), bonus


RWKV_OPERANDS = ("a", "r", "b", "k", "v", "bh", "kh")


def _advance(gens, segments):
    for _ in range(segments):
        for g in gens:
            next(g)


def _finish(gen):
    try:
        while True:
            next(gen)
    except StopIteration as stop:
        return stop.value


def _rwkv_stack(preps, n_pairs):
    lo_mask = _iota((CHUNK, LANES), 1) < HEAD_DIM
    inst = [(d, p) for d in range(len(preps)) for p in range(n_pairs)]
    ops = [[_stack_pair(preps[d][0][name][:, LANES * p:LANES * (p + 1)].astype(BF16), lo_mask) for d, p in inst]
           for name in RWKV_OPERANDS]
    gam = [preps[d][1][:, LANES * p:LANES * (p + 1)] for d, p in inst]
    return ops, gam


def _rwkv_blocks(ops, gam, state_ref, n_dirs, n_sub, n_pairs):
    L = CHUNK
    n2 = 2 * L
    ii = _iota((n2, n2), 0)
    jj = _iota((n2, n2), 1)
    strict = ((jj < ii), (jj > ii))
    incl = ((jj <= ii), (jj >= ii))
    eye = (ii == jj).astype(F32)
    zero = jnp.zeros((n2, n2), F32)
    inst = [(q // n_sub, p) for q in range(n_dirs * n_sub) for p in range(n_pairs)]
    each = lambda f, *ls: [f(*a) for a in zip(*ls)]
    a_s, r_s, b_s, k_s, v_s, bh_s, kh_s = ops
    g = each(lambda a, r, b, k: _dot_nt(jnp.concatenate([a, r], axis=0), jnp.concatenate([b, k], axis=0)),
             a_s, r_s, b_s, k_s)
    n_ab = [jnp.where(strict[d], gi[:n2, :n2], zero) for (d, _), gi in zip(inst, g)]
    a_ak = [jnp.where(strict[d], gi[:n2, n2:], zero).astype(BF16) for (d, _), gi in zip(inst, g)]
    a_rb = [jnp.where(incl[d], gi[n2:, :n2], zero).astype(BF16) for (d, _), gi in zip(inst, g)]
    a_rk = [jnp.where(incl[d], gi[n2:, n2:], zero).astype(BF16) for (d, _), gi in zip(inst, g)]
    yield
    av = each(lambda ak, rk, v: _dot(jnp.concatenate([ak, rk], axis=0), v), a_ak, a_rk, v_s)
    t_inv = [eye + n for n in n_ab]
    npow = each(lambda n: _dot(n, n), [n.astype(BF16) for n in n_ab])
    yield
    for _ in range(CHUNK.bit_length() - 3):
        nb = [n.astype(BF16) for n in npow]
        both = each(lambda t, n: _dot(jnp.concatenate([t.astype(BF16), n], axis=0), n), t_inv, nb)
        t_inv = each(lambda t, x: t + x[:n2], t_inv, both)
        npow = [x[n2:] for x in both]
        yield
    t_inv = each(lambda t, n: t + _dot(t.astype(BF16), n.astype(BF16)), t_inv, npow)
    pq = each(lambda t, a, x: _dot(t.astype(BF16), jnp.concatenate([a, x[:n2].astype(BF16)], axis=1)).astype(BF16),
              t_inv, a_s, av)
    yield
    arb_pq = each(_dot, a_rb, pq)
    ry = each(lambda r, x: (r.astype(F32) + x[:, :LANES]).astype(BF16), r_s, arb_pq)
    y0 = each(lambda x, a: x[:, LANES:] + a[n2:], arb_pq, av)
    yield
    pq_bh = each(_dot_tn, pq, bh_s)
    m_mat = each(lambda gm, x: (eye * gm + x[:LANES]).astype(BF16), gam, pq_bh)
    c_mat = each(lambda x, v, kh: x[LANES:] + _dot_tn(v, kh), pq_bh, v_s, kh_s)
    yield
    state = [state_ref[i] for i in range(n_dirs * n_pairs)]
    y = [None] * len(inst)
    for j in range(n_sub):
        for d in range(n_dirs):
            for p in range(n_pairs):
                i, si = (d * n_sub + j) * n_pairs + p, d * n_pairs + p
                s_old = state[si].astype(BF16)
                y[i] = _dot_nt(ry[i], s_old) + y0[i]
                state[si] = _dot(s_old, m_mat[i]) + c_mat[i]
    for si, s in enumerate(state):
        state_ref[si] = s
    return [jnp.concatenate([y[q * n_pairs + p][:L] + y[q * n_pairs + p][L:] for p in range(n_pairs)], axis=-1)
            for q in range(n_dirs * n_sub)]


def _rwkv_kernel(cur0_ref, prev0_ref, cur1_ref, next1_ref, mu_rkv_ref, mu_wa_ref, w0_ref, wlhi_ref, a0_ref,
                 wllo_ref, kk_ref, ka_ref, rk_ref, bd_ref, y0_ref, y1_ref, b0_ref, b1_ref, ops_ref, gam_ref,
                 state_ref, *, n_pairs, rw_w):
    c = pl.program_id(1)
    L, n_sub = CHUNK, RWKV_SUB

    @pl.when(c == 0)
    def _():
        ops_ref[...] = jnp.zeros_like(ops_ref)
        gam_ref[...] = jnp.zeros_like(gam_ref)
        state_ref[...] = jnp.zeros_like(state_ref)

    n_inst = 2 * n_sub * n_pairs
    staged = [[ops_ref[n, i] for i in range(n_inst)] for n in range(len(RWKV_OPERANDS))]
    staged_gam = [gam_ref[i] for i in range(n_inst)]

    prm = (mu_rkv_ref[...], mu_wa_ref[...], w0_ref[...], wlhi_ref[...], a0_ref[...], wllo_ref[...],
           kk_ref[...], ka_ref[...], rk_ref[...], bd_ref[...])
    first = jnp.minimum(c, pl.num_programs(1) - 2) == 0
    edge0 = jnp.where(first, 0.0, prev0_ref[0, 7:8, :])
    edge1 = jnp.where(first, 0.0, next1_ref[0, 0:1, :])
    x0, x1 = cur0_ref[0], cur1_ref[0]
    rows1 = [(n_sub - 1 - j) * L for j in range(n_sub)]
    prep = ([_rwkv_prep(0, x0[j * L:(j + 1) * L], edge0 if j == 0 else x0[j * L - 1:j * L], prm, rw_w)
             for j in range(n_sub)]
            + [_rwkv_prep(1, x1[r:r + L], edge1 if j == 0 else x1[r + L:r + L + 1], prm, rw_w)
               for j, r in enumerate(rows1)])
    blocks = _rwkv_blocks(staged, staged_gam, state_ref, 2, n_sub, n_pairs)
    _advance([blocks], 2)
    _advance(prep, 1)
    _advance([blocks], 2)
    _advance(prep, 1)
    _advance([blocks], 3)
    preps = [_finish(g) for g in prep]
    b0_ref[0] = jnp.concatenate([preps[j][2] for j in range(n_sub)], axis=0)
    b1_ref[0] = jnp.concatenate([preps[n_sub + j][2] for j in reversed(range(n_sub))], axis=0)
    ops, gam = _rwkv_stack(preps, n_pairs)
    for n, per_inst in enumerate(ops):
        for i, t in enumerate(per_inst):
            ops_ref[n, i] = t
    for i, t in enumerate(gam):
        gam_ref[i] = t

    ys = _finish(blocks)
    y0_ref[0] = jnp.concatenate([ys[j] for j in range(n_sub)], axis=0)
    y1_ref[0] = jnp.concatenate([ys[n_sub + j] for j in reversed(range(n_sub))], axis=0)


def _head_block_diag(width):
    h = jnp.arange(width) // HEAD_DIM
    return (h[:, None] == h[None, :]).astype(F32)


def _rwkv(rw, mu_rkv, mu_w, mu_a, w0, w2, a0, a2, k_k, k_a, r_k, rw_w):
    b, s, cols = rw.shape
    assert s % CHUNK == 0 and rw_w % LANES == 0 and mu_w.shape[-1] == HEAD_DIM and mu_a.shape[-1] == HEAD_DIM
    rows = RWKV_SUB * CHUNK
    assert s % rows == 0
    nc = s // rows
    n_pairs = rw_w // LANES
    sub = rows // 8
    mu_wa = jnp.concatenate([mu_w, mu_a], axis=-1)
    wl = jnp.concatenate([jnp.concatenate([w2, jnp.zeros_like(w2)], axis=1),
                          jnp.concatenate([jnp.zeros_like(a2), a2], axis=1)], axis=2)
    wl_hi, wl_lo = _split2(wl)
    bd = _head_block_diag(rw_w).astype(BF16)
    full = lambda a: pl.BlockSpec(a.shape, lambda bi, c: (0,) * a.ndim)
    params = [mu_rkv, mu_wa, w0, wl_hi, a0, wl_lo, k_k.reshape(1, rw_w), k_a.reshape(1, rw_w),
              r_k.reshape(1, rw_w), bd]
    out = jax.ShapeDtypeStruct((b, s, rw_w), F32)
    cp = lambda c: jnp.minimum(c, nc - 1)
    cm = lambda c: jnp.maximum(c - 1, 0)
    blk = lambda group_of, rev: pl.BlockSpec(
        (1, rows, rw_w), lambda bi, c: (bi, nc - 1 - group_of(c) if rev else group_of(c), 0))
    n_inst = 2 * RWKV_SUB * n_pairs
    return pl.pallas_call(
        functools.partial(_rwkv_kernel, n_pairs=n_pairs, rw_w=rw_w),
        grid=(b, nc + 1),
        in_specs=[pl.BlockSpec((1, rows, cols), lambda bi, c: (bi, cp(c), 0)),
                  pl.BlockSpec((1, 8, cols), lambda bi, c: (bi, jnp.maximum(cp(c) * sub - 1, 0), 0)),
                  pl.BlockSpec((1, rows, cols), lambda bi, c: (bi, nc - 1 - cp(c), 0)),
                  pl.BlockSpec((1, 8, cols),
                               lambda bi, c: (bi, jnp.minimum((nc - cp(c)) * sub, nc * sub - 1), 0))]
        + [full(a) for a in params],
        out_specs=[blk(cm, False), blk(cm, True), blk(cp, False), blk(cp, True)],
        out_shape=[out] * 4,
        scratch_shapes=[pltpu.VMEM((len(RWKV_OPERANDS), n_inst, 2 * CHUNK, LANES), BF16),
                        pltpu.VMEM((n_inst, 1, LANES), F32),
                        pltpu.VMEM((2 * n_pairs, LANES, LANES), F32)],
        compiler_params=_cp(("parallel", "arbitrary")),
    )(rw, rw, rw, rw, *params)


def _memkv_kernel(mem_ref, g_ref, w_ref, mk_ref, mv_ref, *, mem_w):
    x = mem_ref[0]
    ms = jnp.mean(x * x, axis=-1, keepdims=True)
    h = (x * lax.rsqrt(ms + RMS_EPS) * g_ref[...]).astype(BF16)
    kv = _dot(h, w_ref[...])
    mk_ref[0] = kv[:, :mem_w].astype(BF16)
    mv_ref[0] = kv[:, mem_w:].astype(BF16)


def _memkv(mem, g, w_bf, mem_w):
    b, n_mem, d = mem.shape
    out = jax.ShapeDtypeStruct((b, n_mem, mem_w), BF16)
    return pl.pallas_call(
        functools.partial(_memkv_kernel, mem_w=mem_w),
        grid=(b,),
        in_specs=[pl.BlockSpec((1, n_mem, d), lambda i: (i, 0, 0)), pl.BlockSpec((1, d), lambda i: (0, 0)),
                  pl.BlockSpec(w_bf.shape, lambda i: (0, 0))],
        out_specs=[pl.BlockSpec((1, n_mem, mem_w), lambda i: (i, 0, 0))] * 2,
        out_shape=[out, out],
        compiler_params=_cp(("parallel",)),
    )(mem, g, w_bf)


def _merge_kernel(x_ref, yna_ref, y0_ref, y1_ref, b0_ref, b1_ref, latg_ref, memq_ref, mk_ref, mv_ref, gates_ref,
                  lng_ref, lnb_ref, g2_ref, bdm_ref, wna_ref, wrw_ref, wmem_ref, wout_ref, gffn_ref, wrhi_ref,
                  wrlo_ref,
                  x1_ref, h2_ref, aff_ref, *, d_model, mem_pairs):
    tm = x_ref.shape[1]
    lane = _iota((tm, LANES), 1)
    lo_mask = lane < HEAD_DIM
    pair = lambda t, p: t[:, LANES * p:LANES * (p + 1)]
    gates = gates_ref[0]
    ysum = y0_ref[0] + y1_ref[0]
    bdm = bdm_ref[...]
    mean = _dot(ysum.astype(BF16), bdm)
    mq = memq_ref[0] * (HEAD_DIM ** -0.5)
    mk = mk_ref[0]
    mv = mv_ref[0]
    s = [_dot_nt(_stack_pair(pair(mq, p), lo_mask), pair(mk, p)) for p in range(mem_pairs)]
    g_rw = _dot(_sigmoid(latg_ref[0]).astype(BF16), g2_ref[...])
    br_na = _dot(yna_ref[0], wna_ref[...])
    cen = ysum - mean
    var = _dot((cen * cen).astype(BF16), bdm)
    e = [jnp.exp(si - jnp.max(si, axis=-1, keepdims=True)) for si in s]
    o = [_dot(e[p].astype(BF16), pair(mv, p)) / jnp.sum(e[p], axis=-1, keepdims=True) for p in range(mem_pairs)]
    y_mem = jnp.concatenate([jnp.where(lo_mask, oi[:tm], oi[tm:]) for oi in o], axis=-1)
    y = cen * lax.rsqrt(var + GN_EPS) * lng_ref[...] + lnb_ref[...] + (b0_ref[0] + b1_ref[0])
    y_rw = y * g_rw
    merged = (gates[:, :d_model] * br_na
              + gates[:, d_model:2 * d_model] * _dot(y_rw.astype(BF16), wrw_ref[...])
              + gates[:, 2 * d_model:] * _dot(y_mem.astype(BF16), wmem_ref[...]))
    x1 = x_ref[0] + _dot(merged.astype(BF16), wout_ref[...])
    x1_ref[0] = x1
    ms = jnp.mean(x1 * x1, axis=-1, keepdims=True)
    h2 = x1 * lax.rsqrt(ms + RMS_EPS) * gffn_ref[...]
    n_t = d_model // LANES
    for k in range(n_t):
        h2_ref[0, pl.ds(k, tm, stride=n_t), :] = h2[:, LANES * k:LANES * (k + 1)]
    n_exp = aff_ref.shape[1]
    h_hi, h_lo = _split2(h2)
    hh = _dot(jnp.concatenate([h_hi, h_lo], axis=0), wrhi_ref[...])
    logits = hh[:tm] + hh[tm:] + _dot(h_hi, wrlo_ref[...])
    logits = jnp.where(lane < n_exp, logits, NEG_BIG)
    e = jnp.exp(logits - jnp.max(logits, axis=-1, keepdims=True))
    aff = e / jnp.sum(e, axis=-1, keepdims=True)
    aff_ref[0] = aff.T[:n_exp]


def _merge(x, yna, y0, y1, b0, b1, rw, memq, mk, mv, gates, ln_g, ln_b, g2, wna, wrw, wmem, wout, gffn, w_router,
           tm, rw_w):
    b, s, d = x.shape
    n_exp = w_router.shape[1]
    mem_w = memq.shape[-1]
    latg_w = g2.shape[0]
    assert latg_w == LANES and (rw.shape[-1] - latg_w) % LANES == 0 and n_exp <= LANES
    latg_blk = (rw.shape[-1] - latg_w) // LANES
    bdm = (_head_block_diag(rw_w) / HEAD_DIM).astype(BF16)
    wr = jnp.zeros((d, LANES), F32).at[:, :n_exp].set(w_router)
    wr_hi, wr_lo = _split2(wr)
    tok = lambda w: pl.BlockSpec((1, tm, w), lambda bi, j: (bi, j, 0))
    full = lambda a: pl.BlockSpec(a.shape, lambda bi, j: (0,) * a.ndim)
    per_b = lambda a: pl.BlockSpec((1,) + a.shape[1:], lambda bi, j: (bi, 0, 0))
    weights = [ln_g.reshape(1, rw_w), ln_b.reshape(1, rw_w), g2.astype(BF16), bdm, wna, wrw, wmem, wout, gffn,
               wr_hi, wr_lo]
    return pl.pallas_call(
        functools.partial(_merge_kernel, d_model=d, mem_pairs=mem_w // LANES),
        grid=(b, s // tm),
        in_specs=[tok(d), tok(rw_w), tok(rw_w), tok(rw_w), tok(rw_w), tok(rw_w),
                  pl.BlockSpec((1, tm, latg_w), lambda bi, j: (bi, j, latg_blk)),
                  tok(mem_w), per_b(mk), per_b(mv), tok(3 * d)] + [full(a) for a in weights],
        out_specs=[tok(d), pl.BlockSpec((1, tm * (d // LANES), LANES), lambda bi, j: (bi, j, 0)),
                   pl.BlockSpec((1, n_exp, tm), lambda bi, j: (bi, 0, j))],
        out_shape=[jax.ShapeDtypeStruct((b, s, d), F32), jax.ShapeDtypeStruct((b, s * (d // LANES), LANES), F32),
                   jax.ShapeDtypeStruct((b, n_exp, s), F32)],
        compiler_params=_cp(("parallel", "parallel")),
    )(x, yna, y0, y1, b0, b1, rw, memq, mk, mv, gates, *weights)


def _flat_cumsum(m, upper, lower_strict):
    cr = _dot(m.astype(BF16), upper)
    rowtot = jnp.broadcast_to(cr[:, LANES - 1:LANES], cr.shape)
    rowstart = _dot(lower_strict, rowtot.astype(BF16))
    return cr, rowtot, rowstart


def _topk_kernel(aff_ref, idx_ref, val_ref, thr_ref, *, cap):
    n_exp, rows = aff_ref.shape[1], aff_ref.shape[2]
    bits_all = pltpu.bitcast(aff_ref[0], I32)

    def bisect(_, lohi):
        lo, hi = lohi
        mid = lo + ((hi - lo + 1) >> 1)
        cnt = jnp.sum(jnp.sum((bits_all >= mid).astype(I32), axis=2, keepdims=True), axis=1, keepdims=True)
        ok = cnt >= cap
        return jnp.where(ok, mid, lo), jnp.where(ok, hi, mid - 1)

    thr_all, _ = lax.fori_loop(0, 31, bisect, (jnp.zeros((n_exp, 1, 1), I32),
                                               jnp.full((n_exp, 1, 1), 0x7F800000, I32)))
    thr_ref[...] = jnp.broadcast_to(thr_all, thr_ref.shape)

    upper = (_iota((LANES, LANES), 0) <= _iota((LANES, LANES), 1)).astype(BF16)
    lower_strict = (_iota((rows, rows), 1) < _iota((rows, rows), 0)).astype(BF16)
    lower_incl = (_iota((rows, rows), 1) <= _iota((rows, rows), 0)).astype(BF16)
    p_row = _iota((1, cap), 1).astype(F32)

    def per_expert(e, carry):
        aff = aff_ref[0, e]
        bits = pltpu.bitcast(aff, I32)
        thr = thr_ref[e][0:1, :]
        gt = bits > thr
        tie = bits == thr
        need = (cap - jnp.sum(gt.astype(I32))).astype(F32)
        tie_f = tie.astype(F32)
        cr, _, rowstart = _flat_cumsum(tie_f, upper, lower_strict)
        tie_rank = rowstart + cr - tie_f
        sel = (gt | (tie & (tie_rank < need))).astype(F32)

        cr, rowtot, rowstart = _flat_cumsum(sel, upper, lower_strict)
        rowend = _dot(lower_incl, rowtot.astype(BF16))[:, 0:1]
        r_of_p = jnp.sum((rowend <= p_row).astype(I32), axis=0, keepdims=True)
        onehot = (_iota((rows, cap), 0) == r_of_p).astype(BF16)
        pieces = (cr.astype(BF16),) + _split2(rowstart) + _split3(aff)
        g = _dot_tn(jnp.concatenate(pieces, axis=1), onehot)
        blk = lambda n: g[LANES * n:LANES * (n + 1), :]
        q = p_row - (blk(1) + blk(2))[0:1, :]
        jloc = jnp.sum((blk(0) <= q).astype(I32), axis=0, keepdims=True)
        idx_ref[0, pl.ds(e, 1), :] = r_of_p * LANES + jloc
        val_ref[0, pl.ds(e, 1), :] = jnp.sum(
            jnp.where(_iota((LANES, cap), 0) == jloc, blk(3) + blk(4) + blk(5), 0.0), axis=0, keepdims=True)
        return carry

    lax.fori_loop(0, n_exp, per_expert, 0)


def _topk(aff_t, cap):
    b, n_exp, s = aff_t.shape
    assert s % LANES == 0
    rows = s // LANES
    a4 = aff_t.reshape(b, n_exp, rows, LANES)
    spec = pl.BlockSpec((1, n_exp, cap), lambda bi: (bi, 0, 0))
    return pl.pallas_call(
        functools.partial(_topk_kernel, cap=cap),
        grid=(b,),
        in_specs=[pl.BlockSpec((1, n_exp, rows, LANES), lambda bi: (bi, 0, 0, 0))],
        out_specs=[spec, spec],
        out_shape=[jax.ShapeDtypeStruct((b, n_exp, cap), I32), jax.ShapeDtypeStruct((b, n_exp, cap), F32)],
        scratch_shapes=[pltpu.VMEM((n_exp, 8, LANES), I32)],
        compiler_params=_cp(("parallel",)),
    )(a4)


def _gather_kernel(idx_ref, h_ref, o_ref, *, cap, n_exp, n_t):
    base = (pl.program_id(0) * n_exp + pl.program_id(1)) * cap
    row = lambda r: pl.ds(pl.multiple_of(r * n_t, n_t), n_t)

    def body(p, carry):
        o_ref[0, 0, row(p), :] = h_ref[0, row(idx_ref[base + p]), :]
        return carry

    lax.fori_loop(0, cap, body, 0, unroll=8)


def _gather(idx_flat, h_tiles, n_exp, cap, n_t):
    b, rows, _ = h_tiles.shape
    assert n_t == 8
    return pl.pallas_call(
        functools.partial(_gather_kernel, cap=cap, n_exp=n_exp, n_t=n_t),
        grid_spec=pltpu.PrefetchScalarGridSpec(
            num_scalar_prefetch=1, grid=(b, n_exp),
            in_specs=[pl.BlockSpec((1, rows, LANES), lambda bi, e, idx: (bi, 0, 0), pipeline_mode=pl.Buffered(1))],
            out_specs=pl.BlockSpec((1, 1, cap * n_t, LANES), lambda bi, e, idx: (bi, e, 0, 0))),
        out_shape=jax.ShapeDtypeStruct((b, n_exp, cap * n_t, LANES), F32),
        compiler_params=_cp(("parallel", "arbitrary")),
    )(idx_flat, h_tiles)


def _ffn_kernel(xe_ref, wg_ref, wu_ref, wd_ref, val_ref, o_ref, acc_ref, xb_ref):
    f = pl.program_id(2)

    @pl.when(f == 0)
    def _():
        acc_ref[...] = jnp.zeros_like(acc_ref)
        n_t = xb_ref.shape[1] // LANES
        rows = xb_ref.shape[0]
        for k in range(n_t):
            xb_ref[:, LANES * k:LANES * (k + 1)] = xe_ref[0, 0, pl.ds(k, rows, stride=n_t), :].astype(BF16)

    xb = xb_ref[...]
    w = wg_ref.shape[2] // FFN_PARTS
    part = lambda j: slice(j * w, (j + 1) * w)
    gu = [(_dot(xb, wg_ref[0, :, part(j)].astype(BF16)), _dot(xb, wu_ref[0, :, part(j)].astype(BF16)))
          for j in range(FFN_PARTS)]
    act = [(g * _sigmoid(g) * u).astype(BF16) for g, u in gu]
    acc_ref[...] += sum(_dot(a, wd_ref[0, part(j), :].astype(BF16)) for j, a in enumerate(act))

    @pl.when(f == pl.num_programs(2) - 1)
    def _():
        o = acc_ref[...] * val_ref[0, 0]
        cap, d = o.shape
        n_t = d // LANES
        for k in range(n_t):
            o_ref[0, 0, pl.ds(k, cap, stride=n_t), :] = o[:, LANES * k:LANES * (k + 1)]


def _ffn(xe, w_gate, w_up, w_down, val, fch):
    b, n_exp, cap = val.shape[:3]
    d = w_gate.shape[1]
    ff = w_gate.shape[-1]
    assert ff % fch == 0
    return pl.pallas_call(
        _ffn_kernel,
        grid=(n_exp, b, ff // fch),
        in_specs=[pl.BlockSpec((1, 1, cap * (d // LANES), LANES), lambda e, bi, f: (bi, e, 0, 0)),
                  pl.BlockSpec((1, d, fch), lambda e, bi, f: (e, 0, f)),
                  pl.BlockSpec((1, d, fch), lambda e, bi, f: (e, 0, f)),
                  pl.BlockSpec((1, fch, d), lambda e, bi, f: (e, f, 0)),
                  pl.BlockSpec((1, 1, cap, 1), lambda e, bi, f: (bi, e, 0, 0))],
        out_specs=pl.BlockSpec((1, 1, cap * (d // LANES), LANES), lambda e, bi, f: (bi, e, 0, 0)),
        out_shape=jax.ShapeDtypeStruct((b, n_exp, cap * (d // LANES), LANES), F32),
        scratch_shapes=[pltpu.VMEM((cap, d), F32), pltpu.VMEM((cap, d), BF16)],
        compiler_params=_cp(("parallel", "parallel", "arbitrary")),
    )(xe, w_gate, w_up, w_down, val)


def _combine_kernel(idx_ref, ye_ref, o_ref, *, cap, n_exp, tq, n_t):
    q = pl.program_id(1)
    e = pl.program_id(2)

    @pl.when(e == 0)
    def _():
        o_ref[...] = jnp.zeros_like(o_ref)

    base = (pl.program_id(0) * n_exp + e) * cap
    t0 = q * tq

    def lower_bound(t):
        def step(_, lohi):
            lo, hi = lohi
            live = lo < hi
            mid = (lo + hi) >> 1
            less = idx_ref[base + jnp.minimum(mid, cap - 1)] < t
            return jnp.where(live & less, mid + 1, lo), jnp.where(live & ~less, mid, hi)

        return lax.fori_loop(0, cap.bit_length(), step, (jnp.int32(0), jnp.int32(cap)))[0]

    lo = lower_bound(t0)
    hi = lower_bound(t0 + tq)
    row = lambda r: pl.ds(pl.multiple_of(r * n_t, n_t), n_t)
    n_groups = (hi - lo) // SCATTER_GROUP

    def group(g, carry):
        p0 = lo + g * SCATTER_GROUP
        toks = [idx_ref[base + p0 + u] - t0 for u in range(SCATTER_GROUP)]
        new = [o_ref[0, row(i), :] + ye_ref[0, 0, row(p0 + u), :] for u, i in enumerate(toks)]
        for i, r in zip(toks, new):
            o_ref[0, row(i), :] = r
        return carry

    lax.fori_loop(0, n_groups, group, 0)

    def single(p, carry):
        i = idx_ref[base + p] - t0
        o_ref[0, row(i), :] += ye_ref[0, 0, row(p), :]
        return carry

    lax.fori_loop(lo + n_groups * SCATTER_GROUP, hi, single, 0)


def _combine(idx_flat, ye, cap, s, tq):
    b, n_exp, rows, _ = ye.shape
    n_t = rows // cap
    assert s % tq == 0 and n_t == 8
    return pl.pallas_call(
        functools.partial(_combine_kernel, cap=cap, n_exp=n_exp, tq=tq, n_t=n_t),
        grid_spec=pltpu.PrefetchScalarGridSpec(
            num_scalar_prefetch=1, grid=(b, s // tq, n_exp),
            in_specs=[pl.BlockSpec((1, 1, rows, LANES), lambda bi, q, e, idx: (bi, e, 0, 0))],
            out_specs=pl.BlockSpec((1, tq * n_t, LANES), lambda bi, q, e, idx: (bi, q, 0),
                                   pipeline_mode=pl.Buffered(1))),
        out_shape=jax.ShapeDtypeStruct((b, s * n_t, LANES), F32),
        compiler_params=_cp(("parallel", "parallel", "arbitrary")),
    )(idx_flat, ye)


def _final_kernel(x1_ref, moe_ref, g_ref, o_ref):
    tm, d = x1_ref.shape
    n_t = d // LANES
    moe = jnp.concatenate([moe_ref[pl.ds(k, tm, stride=n_t), :] for k in range(n_t)], axis=-1)
    x = x1_ref[...] + moe
    ms = jnp.mean(x * x, axis=-1, keepdims=True)
    o_ref[...] = x * lax.rsqrt(ms + RMS_EPS) * g_ref[...]


def _final(x1, moe_tiles, g, tm):
    n, d = x1.shape
    n_t = d // LANES
    row = pl.BlockSpec((tm, d), lambda i: (i, 0))
    return pl.pallas_call(
        _final_kernel,
        grid=(n // tm,),
        in_specs=[row, pl.BlockSpec((tm * n_t, LANES), lambda i: (i, 0)), pl.BlockSpec((1, d), lambda i: (0, 0))],
        out_specs=row,
        out_shape=jax.ShapeDtypeStruct((n, d), F32),
        compiler_params=_cp(("parallel",)),
    )(x1, moe_tiles, g)


def _tiles(s, ff):
    return dict(
        tm=min(256, s),
        tm_final=min(512, s),
        na_rows=min(16, s // GRID_W),
        ff_chunk=min(1024, ff),
        tq=s,
    )


def _layer(x, mem, p, g_final):
    b, s, d = x.shape
    na_w = p["w_branch_na"].shape[0]
    rw_w = p["w_branch_rw"].shape[0]
    mem_w = p["w_branch_mem"].shape[0]
    n_exp = p["w_router"].shape[1]
    cap = EC_CAPACITY * s // n_exp
    rw_cols = p["w_in"].shape[1] - 3 * na_w - mem_w - 3 * d
    t = _tiles(s, p["w_exp_gate"].shape[-1])
    bf = lambda name: p[name].astype(BF16)

    q, k, v, rw, memq, gates = _inproj(x.reshape(b * s, d), p["norm_mix_g"].reshape(1, d), bf("w_in"),
                                       na_w, rw_cols, mem_w, d, t["tm"])
    shp = lambda a: a.reshape(b, s, a.shape[-1])
    q, k, v, rw, memq, gates = map(shp, (q, k, v, rw, memq, gates))
    y_na = _na_attention(q, k, v, _na_bias_table(p["na_rpb"]), tr=t["na_rows"])
    y0, y1, b0, b1 = _rwkv(rw, p["rw_mu_rkv"], p["rw_mu_w"], p["rw_mu_a"], p["rw_w0"], p["rw_w2"], p["rw_a0"],
                           p["rw_a2"], p["rw_k_k"], p["rw_k_a"], p["rw_r_k"], rw_w)
    mk, mv = _memkv(mem, p["norm_mem_g"].reshape(1, d), bf("w_mem_kv"), mem_w)
    x1, h2p, aff_t = _merge(x, y_na, y0, y1, b0, b1, rw, memq, mk, mv, gates, p["rw_ln_g"], p["rw_ln_b"],
                            p["rw_g2"], bf("w_branch_na"), bf("w_branch_rw"), bf("w_branch_mem"), bf("w_out"),
                            p["norm_ffn_g"].reshape(1, d), p["w_router"], t["tm"], rw_w)
    idx, val = _topk(aff_t, cap)
    idx_flat = idx.reshape(b * n_exp * cap)
    xe = _gather(idx_flat, h2p, n_exp, cap, d // LANES)
    ye = _ffn(xe, p["w_exp_gate"], p["w_exp_up"], p["w_exp_down"], val.reshape(b, n_exp, cap, 1), t["ff_chunk"])
    moe = _combine(idx_flat, ye, cap, s, t["tq"])
    return _final(x1.reshape(b * s, d), moe.reshape(b * s * (d // LANES), LANES), g_final.reshape(1, d),
                  t["tm_final"]).reshape(b, s, d)


def kernel(x, mem, norm_mix_g, norm_mem_g, w_in, na_rpb, rw_mu_rkv, rw_mu_w, rw_mu_a, rw_w0, rw_w2, rw_a0, rw_a2,
           rw_k_k, rw_k_a, rw_r_k, rw_g2, rw_ln_g, rw_ln_b, w_mem_kv, w_branch_na, w_branch_rw, w_branch_mem, w_out,
           norm_ffn_g, w_router, w_exp_gate, w_exp_up, w_exp_down, norm_final_g):
    stacked = dict(norm_mix_g=norm_mix_g, norm_mem_g=norm_mem_g, w_in=w_in, na_rpb=na_rpb, rw_mu_rkv=rw_mu_rkv,
                   rw_mu_w=rw_mu_w, rw_mu_a=rw_mu_a, rw_w0=rw_w0, rw_w2=rw_w2, rw_a0=rw_a0, rw_a2=rw_a2,
                   rw_k_k=rw_k_k, rw_k_a=rw_k_a, rw_r_k=rw_r_k, rw_g2=rw_g2, rw_ln_g=rw_ln_g, rw_ln_b=rw_ln_b,
                   w_mem_kv=w_mem_kv, w_branch_na=w_branch_na, w_branch_rw=w_branch_rw, w_branch_mem=w_branch_mem,
                   w_out=w_out, norm_ffn_g=norm_ffn_g, w_router=w_router, w_exp_gate=w_exp_gate,
                   w_exp_up=w_exp_up, w_exp_down=w_exp_down)
    assert w_in.shape[0] == 1, "only a depth-1 stack is supported"
    return _layer(x, mem, {name: a[0] for name, a in stacked.items()}, norm_final_g)
```
